```python
import math
import jax, jax.numpy as jnp
from jax import lax
import numpy as np

D_MODEL = 1024
BATCH = 32
SEQ = 2048
DEPTH = 1

CHUNK = 64
Q_BLOCK = 128
A_HEADS = 8
A_HEAD_DIM = 64
A_Q = A_HEADS * A_HEAD_DIM
A_LATENT = 128
IDX_HEADS = 8
IDX_DIM = 64
TOPK_MAX = 256
REL_BUCKETS = 32
REL_MAX_DIST = 128
B_HEADS = 4
B_KEY_DIM = 128
B_VAL_DIM = 128
B_K = B_HEADS * B_KEY_DIM
B_V = B_HEADS * B_VAL_DIM
PEER_HEADS = 8
PEER_KEYS = 128
PEER_EXPERTS = PEER_KEYS * PEER_KEYS
PEER_QDIM = 256
PEER_TOPK = 16
PEER_TOKEN_BLOCK = 512
ALPHA = (2 * DEPTH) ** 0.25
BETA = (8 * DEPTH) ** -0.25
LN_EPS = 1e-5
RMS_EPS = 1e-6

IN_SPLITS = (A_Q, A_LATENT, IDX_HEADS * IDX_DIM, IDX_DIM, IDX_HEADS, B_K, B_K, B_V, B_V, D_MODEL, D_MODEL)
IN_COLS = A_Q + A_LATENT + IDX_HEADS * IDX_DIM + IDX_DIM + IDX_HEADS + 2 * B_K + 2 * B_V + 2 * D_MODEL

kernel_name = "hybrid_dsa_hgrn2_peer_deepnorm"


def layer_norm(x, g, b):
    xf = x.astype(jnp.float32)
    mu = jnp.mean(xf, -1, keepdims=True)
    var = jnp.mean(jnp.square(xf - mu), -1, keepdims=True)
    return ((xf - mu) * lax.rsqrt(var + LN_EPS) * g + b).astype(x.dtype)


def rms_norm(x, g):
    xf = x.astype(jnp.float32)
    return (xf * lax.rsqrt(jnp.mean(xf * xf, -1, keepdims=True) + RMS_EPS) * g).astype(x.dtype)


def t5_bucket(rel):
    half = REL_BUCKETS // 2
    max_exact = half // 2
    base = jnp.where(rel > 0, half, 0)
    n = jnp.abs(rel)
    nf = jnp.maximum(n, 1).astype(jnp.float32)
    large = max_exact + (jnp.log(nf / max_exact) / math.log(REL_MAX_DIST / max_exact) * (half - max_exact)).astype(jnp.int32)
    large = jnp.minimum(large, half - 1)
    return base + jnp.where(n < max_exact, n, large)


def dsa_mixer(q, c_kv, q_idx, k_idx, w_idx, w_uk, w_uv, rel_bias):
    B, S = q.shape[0], q.shape[1]
    n_sel = min(TOPK_MAX, S // 4)
    nblk = S // Q_BLOCK
    q_lat = jnp.einsum('bshd,chd->bshc', q, w_uk)
    key_chunk = jnp.arange(S) // CHUNK
    scale = A_HEAD_DIM ** -0.5

    def to_blocks(t):
        return jnp.moveaxis(t.reshape((B, nblk, Q_BLOCK) + t.shape[2:]), 1, 0)

    def block(args):
        ql, qi, wi, qpos = args
        q_chunk = qpos // CHUNK
        logits = jnp.einsum('bthd,bsd->bths', qi, k_idx)
        score = jnp.einsum('bths,bth->bts', jax.nn.relu(logits), wi).astype(jnp.float32)
        visible = key_chunk[None, :] <= q_chunk[:, None]
        score = jnp.where(visible[None], score, -jnp.inf)
        _, idx = lax.top_k(score, n_sel)
        valid = (idx // CHUNK) <= q_chunk[None, :, None]
        c_sel = jax.vmap(lambda c, i: c[i])(c_kv, idx.reshape(B, -1)).reshape(B, Q_BLOCK, n_sel, A_LATENT)
        bias = rel_bias[t5_bucket(idx - qpos[None, :, None])]
        s = jnp.einsum('bthc,btkc->bthk', ql, c_sel).astype(jnp.float32) * scale + jnp.moveaxis(bias, -1, 2)
        s = jnp.where(valid[:, :, None, :], s, -jnp.inf)
        p = jax.nn.softmax(s, axis=-1).astype(c_sel.dtype)
        return jnp.einsum('bthk,btkc->bthc', p, c_sel)

    pos = jnp.arange(S).reshape(nblk, Q_BLOCK)
    o_lat = lax.map(block, (to_blocks(q_lat), to_blocks(q_idx), to_blocks(w_idx), pos))
    o_lat = jnp.moveaxis(o_lat, 0, 1).reshape(B, S, A_HEADS, A_LATENT)
    o = jnp.einsum('bshc,chd->bshd', o_lat, w_uv)
    return o.reshape(B, S, A_Q)


def hgrn2_mixer(q, f_logit, i, g, lb, norm_g):
    B, S, H, dk = q.shape
    dv = i.shape[-1]
    n = S // CHUNK
    f32 = jnp.float32
    lbh = lb.reshape(H, dk).astype(f32)
    z = f_logit.astype(f32)
    log_f = jnp.log(lbh + (1.0 - lbh) * jax.nn.sigmoid(z))
    k = (1.0 - lbh) * jax.nn.sigmoid(-z)

    def chunks(t):
        return t.astype(f32).reshape(B, n, CHUNK, H, t.shape[-1]).transpose(0, 3, 1, 2, 4)

    qc, kc, vc = chunks(q), chunks(k), chunks(i)
    bc = jnp.cumsum(chunks(log_f), axis=3)
    b_ref = bc[..., CHUNK // 2 - 1:CHUNK // 2, :]
    q_in = qc * jnp.exp(bc - b_ref)
    k_in = kc * jnp.exp(b_ref - bc)
    causal = jnp.tril(jnp.ones((CHUNK, CHUNK), dtype=bool))
    attn = jnp.where(causal, jnp.einsum('bhntd,bhnsd->bhnts', q_in, k_in), 0.0)
    o_intra = jnp.einsum('bhnts,bhnsv->bhntv', attn, vc)
    b_last = bc[..., -1:, :]
    dS = jnp.einsum('bhnsd,bhnsv->bhndv', kc * jnp.exp(b_last - bc), vc)
    decay = jnp.exp(b_last[..., 0, :])

    def step(state, xs):
        dS_n, dec_n = xs
        return dec_n[..., None] * state + dS_n, state

    _, S_before = lax.scan(step, jnp.zeros((B, H, dk, dv), f32), (jnp.moveaxis(dS, 2, 0), jnp.moveaxis(decay, 2, 0)))
    S_before = jnp.moveaxis(S_before, 0, 2)
    o_inter = jnp.einsum('bhntd,bhndv->bhntv', qc * jnp.exp(bc), S_before)
    o = (o_intra + o_inter).transpose(0, 2, 3, 1, 4).reshape(B, S, H, dv)
    gf = g.astype(f32)
    o = rms_norm(o, norm_g) * gf * jax.nn.sigmoid(gf)
    return o.reshape(B, S, H * dv).astype(q.dtype)


def mixer_sublayer(x, w_in, kv_norm_g, w_uk, w_uv, rel_bias, lb, b_norm_g, w_br_a, w_br_b, w_o):
    B, S, _ = x.shape
    proj = x @ w_in
    offs = np.cumsum(np.array(IN_SPLITS))[:-1].tolist()
    qa, ckv, qi, ki, wi, qb, fb, ib, gb, gate_a, gate_b = jnp.split(proj, offs, axis=-1)
    ya = dsa_mixer(qa.reshape(B, S, A_HEADS, A_HEAD_DIM), rms_norm(ckv, kv_norm_g),
                   qi.reshape(B, S, IDX_HEADS, IDX_DIM), ki, wi * (IDX_HEADS * IDX_DIM) ** -0.5,
                   w_uk, w_uv, rel_bias) @ w_br_a
    yb = hgrn2_mixer(qb.reshape(B, S, B_HEADS, B_KEY_DIM), fb.reshape(B, S, B_HEADS, B_KEY_DIM),
                     ib.reshape(B, S, B_HEADS, B_VAL_DIM), gb.reshape(B, S, B_HEADS, B_VAL_DIM),
                     lb, b_norm_g) @ w_br_b
    merged = jax.nn.sigmoid(gate_a) * ya + jax.nn.sigmoid(gate_b) * yb
    return merged @ w_o


def peer_sublayer(x, w_pq, sub_keys1, sub_keys2, u_table, v_table):
    B, S, D = x.shape
    T = B * S
    blk = math.gcd(T, PEER_TOKEN_BLOCK)

    def block(xb):
        q = (xb @ w_pq).reshape(blk, PEER_HEADS, 2, PEER_QDIM // 2)
        s1 = jnp.einsum('thd,nd->thn', q[:, :, 0], sub_keys1).astype(jnp.float32)
        s2 = jnp.einsum('thd,nd->thn', q[:, :, 1], sub_keys2).astype(jnp.float32)
        v1, i1 = lax.top_k(s1, PEER_TOPK)
        v2, i2 = lax.top_k(s2, PEER_TOPK)
        cand = (v1[..., :, None] + v2[..., None, :]).reshape(blk, PEER_HEADS, PEER_TOPK * PEER_TOPK)
        cand_id = (i1[..., :, None] * PEER_KEYS + i2[..., None, :]).reshape(blk, PEER_HEADS, PEER_TOPK * PEER_TOPK)
        top_s, pos = lax.top_k(cand, PEER_TOPK)
        ids = jnp.take_along_axis(cand_id, pos, axis=-1).reshape(blk, PEER_HEADS * PEER_TOPK)
        gate = jax.nn.softmax(top_s, axis=-1).reshape(blk, PEER_HEADS * PEER_TOPK).astype(xb.dtype)
        u = jnp.take(u_table, ids, axis=0)
        h = jax.nn.gelu(jnp.einsum('tkd,td->tk', u, xb), approximate=False)
        v = jnp.take(v_table, ids, axis=0)
        return jnp.einsum('tk,tkd->td', gate * h, v)

    out = lax.map(block, x.reshape(T // blk, blk, D))
    return out.reshape(B, S, D)


def setup_inputs(seed: int = 0) -> dict:
    key = jax.random.key(seed)
    ks = jax.random.split(key, 21)
    L = DEPTH

    def nrm(k, shape, scale):
        return jax.random.normal(k, shape, jnp.float32) * scale

    return {
        "x": nrm(ks[0], (BATCH, SEQ, D_MODEL), 1.0),
        "w_in": nrm(ks[1], (L, D_MODEL, IN_COLS), D_MODEL ** -0.5),
        "kv_norm_g": 1.0 + nrm(ks[2], (L, A_LATENT), 0.02),
        "w_uk": nrm(ks[3], (L, A_LATENT, A_HEADS, A_HEAD_DIM), A_LATENT ** -0.5),
        "w_uv": nrm(ks[4], (L, A_LATENT, A_HEADS, A_HEAD_DIM), A_LATENT ** -0.5),
        "rel_bias": nrm(ks[5], (REL_BUCKETS, A_HEADS), 0.5),
        "lb_params": 1.0 + nrm(ks[6], (DEPTH + 1, B_K), 0.1),
        "b_norm_g": 1.0 + nrm(ks[7], (L, B_VAL_DIM), 0.02),
        "w_br_a": nrm(ks[8], (L, A_Q, D_MODEL), A_Q ** -0.5 * BETA),
        "w_br_b": nrm(ks[9], (L, B_V, D_MODEL), B_V ** -0.5 * BETA),
        "w_o": nrm(ks[10], (L, D_MODEL, D_MODEL), D_MODEL ** -0.5 * BETA),
        "ln1_g": 1.0 + nrm(ks[11], (L, D_MODEL), 0.02),
        "ln1_b": nrm(ks[12], (L, D_MODEL), 0.01),
        "w_pq": nrm(ks[13], (L, D_MODEL, PEER_HEADS * PEER_QDIM), D_MODEL ** -0.5),
        "sub_keys1": nrm(ks[14], (L, PEER_KEYS, PEER_QDIM // 2), (PEER_QDIM // 2) ** -0.5),
        "sub_keys2": nrm(ks[15], (L, PEER_KEYS, PEER_QDIM // 2), (PEER_QDIM // 2) ** -0.5),
        "u_table": nrm(ks[16], (L, PEER_EXPERTS, D_MODEL), D_MODEL ** -0.5),
        "v_table": nrm(ks[17], (L, PEER_EXPERTS, D_MODEL), BETA * PEER_HEADS ** -0.5),
        "ln2_g": 1.0 + nrm(ks[18], (L, D_MODEL), 0.02),
        "ln2_b": nrm(ks[19], (L, D_MODEL), 0.01),
    }


def reference(x, w_in, kv_norm_g, w_uk, w_uv, rel_bias, lb_params, b_norm_g, w_br_a, w_br_b, w_o,
              ln1_g, ln1_b, w_pq, sub_keys1, sub_keys2, u_table, v_table, ln2_g, ln2_b):
    lower_bounds = jnp.cumsum(jax.nn.softmax(lb_params.astype(jnp.float32), axis=0), axis=0)
    for l in range(DEPTH):
        mix = mixer_sublayer(x, w_in[l], kv_norm_g[l], w_uk[l], w_uv[l], rel_bias, lower_bounds[l],
                             b_norm_g[l], w_br_a[l], w_br_b[l], w_o[l])
        x = layer_norm(ALPHA * x + mix, ln1_g[l], ln1_b[l])
        ffn = peer_sublayer(x, w_pq[l], sub_keys1[l], sub_keys2[l], u_table[l], v_table[l])
        x = layer_norm(ALPHA * x + ffn, ln2_g[l], ln2_b[l])
    return x
```

```python
import functools
import math

import numpy as np
import jax
import jax.numpy as jnp
from jax import lax
from jax.experimental import pallas as pl
from jax.experimental.pallas import tpu as pltpu

f32 = jnp.float32
bf16 = jnp.bfloat16
i32 = jnp.int32
HIGHEST = lax.Precision.HIGHEST

D_MODEL = 1024
DEPTH = 1
CHUNK = 64
Q_BLOCK = 128
A_HEADS = 8
A_HEAD_DIM = 64
A_Q = A_HEADS * A_HEAD_DIM
A_LATENT = 128
IDX_HEADS = 8
IDX_DIM = 64
TOPK_MAX = 256
REL_BUCKETS = 32
REL_MAX_DIST = 128
B_HEADS = 4
B_KEY_DIM = 128
B_VAL_DIM = 128
B_K = B_HEADS * B_KEY_DIM
B_V = B_HEADS * B_VAL_DIM
PEER_HEADS = 8
PEER_KEYS = 128
PEER_QDIM = 256
PEER_TOPK = 16
ALPHA = (2 * DEPTH) ** 0.25
LN_EPS = 1e-5
RMS_EPS = 1e-6

IN_SPLITS = (A_Q, A_LATENT, IDX_HEADS * IDX_DIM, IDX_DIM, IDX_HEADS, B_K, B_K, B_V, B_V, D_MODEL, D_MODEL)

COL_QA, COL_QI, COL_QB, COL_FB, COL_IB, COL_GB = 0, 512, 1024, 1536, 2048, 2560
COL_GATE_A, COL_GATE_B, COL_CKV, COL_KIWI = 3072, 4096, 5120, 5248
PROJ_COLS = 5376
LANES = 128
ROWS_PER_EXPERT = 4

INT_MIN = -(2 ** 31)
NEG_BIG = -1e30
VMEM_LIMIT = 48 * 1024 * 1024
TABLE_VMEM_LIMIT = 56 * 1024 * 1024

_NT = (((1,), (1,)), ((), ()))


def _params(*sem, vmem=VMEM_LIMIT):
    return pltpu.CompilerParams(dimension_semantics=sem, vmem_limit_bytes=vmem)


def _proj_body(x_ref, w_ref, o_ref):
    o_ref[...] = jnp.dot(x_ref[...].astype(bf16), w_ref[...], preferred_element_type=f32)


def _project(x2, w, tm=256):
    T = x2.shape[0]
    return pl.pallas_call(
        _proj_body,
        grid=(T // tm,),
        in_specs=[pl.BlockSpec((tm, D_MODEL), lambda i: (i, 0)),
                  pl.BlockSpec((D_MODEL, PROJ_COLS), lambda i: (0, 0))],
        out_specs=pl.BlockSpec((tm, PROJ_COLS), lambda i: (i, 0)),
        out_shape=jax.ShapeDtypeStruct((T, PROJ_COLS), f32),
        compiler_params=_params("arbitrary"),
        name="project",
    )(x2, w)


def _hgrn_body(q_ref, f_ref, i_ref, g_ref, lb_ref, ng_ref, o_ref, st_ref, *, n_chunks):
    @pl.when(pl.program_id(1) == 0)
    def _():
        st_ref[...] = jnp.zeros_like(st_ref)

    lb = lb_ref[...]
    ng = ng_ref[...]
    row = lax.broadcasted_iota(i32, (CHUNK, CHUNK), 0)
    col = lax.broadcasted_iota(i32, (CHUNK, CHUNK), 1)
    causal = row >= col
    tri = causal.astype(f32)
    for n in range(n_chunks):
        sl = pl.ds(n * CHUNK, CHUNK)
        z = f_ref[sl, :]
        log_f = jnp.log(lb + (1.0 - lb) * jax.nn.sigmoid(z))
        kk = (1.0 - lb) * jax.nn.sigmoid(-z)
        bc = jnp.dot(tri, log_f, precision=HIGHEST, preferred_element_type=f32)
        b_mid = bc[CHUNK // 2 - 1:CHUNK // 2, :]
        b_last = bc[CHUNK - 1:CHUNK, :]
        q = q_ref[sl, :]
        v = i_ref[sl, :]
        g = g_ref[sl, :]
        q_in = (q * jnp.exp(bc - b_mid)).astype(bf16)
        k_in = (kk * jnp.exp(b_mid - bc)).astype(bf16)
        k_dec = kk * jnp.exp(b_last - bc)
        q_dec = (q * jnp.exp(bc)).astype(bf16)
        decay = jnp.exp(b_last)
        outs = []
        for h in range(B_HEADS):
            hs = slice(h * B_KEY_DIM, (h + 1) * B_KEY_DIM)
            attn = lax.dot_general(q_in[:, hs], k_in[:, hs], _NT, preferred_element_type=f32)
            attn = jnp.where(causal, attn, 0.0)
            vh = v[:, hs]
            o = jnp.dot(attn.astype(bf16), vh.astype(bf16), preferred_element_type=f32)
            st = st_ref[h]
            o = o + lax.dot_general(q_dec[:, hs], st.astype(bf16), _NT, preferred_element_type=f32)
            d_st = jnp.dot(vh.T.astype(bf16), k_dec[:, hs].astype(bf16), preferred_element_type=f32)
            st_ref[h] = decay[:, hs] * st + d_st
            ms = jnp.mean(o * o, axis=-1, keepdims=True)
            gh = g[:, hs]
            outs.append(o * lax.rsqrt(ms + RMS_EPS) * ng * gh * jax.nn.sigmoid(gh))
        o_ref[sl, :] = jnp.concatenate(outs, axis=1)


def _hgrn(proj, lb, norm_g, B, S, tc=256):
    nc = S // tc
    blk = lambda c: pl.BlockSpec((tc, B_K), lambda b, s, c=c: (b * nc + s, c))
    return pl.pallas_call(
        functools.partial(_hgrn_body, n_chunks=tc // CHUNK),
        grid=(B, nc),
        in_specs=[blk(COL_QB // B_K), blk(COL_FB // B_K), blk(COL_IB // B_K), blk(COL_GB // B_K),
                  pl.BlockSpec((1, B_K), lambda b, s: (0, 0)),
                  pl.BlockSpec((1, B_VAL_DIM), lambda b, s: (0, 0))],
        out_specs=pl.BlockSpec((tc, B_V), lambda b, s: (b * nc + s, 0)),
        out_shape=jax.ShapeDtypeStruct((B * S, B_V), f32),
        scratch_shapes=[pltpu.VMEM((B_HEADS, B_VAL_DIM, B_KEY_DIM), f32)],
        compiler_params=_params("arbitrary", "arbitrary"),
        name="hgrn",
    )(proj, proj, proj, proj, lb, norm_g)


def _dsa_body(qa_ref, qi_ref, kwq_ref, ckv_ref, kw_ref, g_ref, wuk_ref, wuv_ref, bt_ref, bfar_ref, o_ref,
              ckvn_ref, skey_ref, m_ref, l_ref, acc_ref, qlat_ref, qih_ref, wq_ref, thr_ref, ilim_ref,
              *, S, n_sel):
    j = pl.program_id(1)
    QB = Q_BLOCK
    n_kb = S // QB
    scale = A_HEAD_DIM ** -0.5
    rowi = lax.broadcasted_iota(i32, (QB, QB), 0)
    coli = lax.broadcasted_iota(i32, (QB, QB), 1)

    @pl.when(j == 0)
    def _():
        def norm_blk(kb, c):
            sl = pl.ds(pl.multiple_of(kb * QB, QB), QB)
            cb = ckv_ref[sl, :]
            ms = jnp.mean(cb * cb, axis=-1, keepdims=True)
            ckvn_ref[sl, :] = (cb * lax.rsqrt(ms + RMS_EPS) * g_ref[...]).astype(bf16)
            return c
        lax.fori_loop(0, n_kb, norm_blk, 0)

    qa = qa_ref[...]
    qi = qi_ref[...]
    wq = kwq_ref[:, IDX_DIM:IDX_DIM + IDX_HEADS] * ((IDX_HEADS * IDX_DIM) ** -0.5)
    for h in range(A_HEADS):
        qh = qa[:, h * A_HEAD_DIM:(h + 1) * A_HEAD_DIM].astype(bf16)
        qlat_ref[h] = jnp.dot(qh, wuk_ref[h], preferred_element_type=f32).astype(bf16)
    for h in range(IDX_HEADS):
        qih_ref[h] = qi[:, h * IDX_DIM:(h + 1) * IDX_DIM].astype(bf16)
        wq_ref[h] = jnp.broadcast_to(wq[:, h:h + 1], (QB, QB))

    def index_block(kb, is_current):
        sl = pl.ds(pl.multiple_of(kb * QB, QB), QB)
        kblk = kw_ref[sl, :][:, :IDX_DIM].astype(bf16)
        sc = jnp.zeros((QB, QB), f32)
        for h in range(IDX_HEADS):
            lg = lax.dot_general(qih_ref[h], kblk, _NT, preferred_element_type=f32)
            sc = sc + wq_ref[h] * jnp.maximum(lg, 0.0)
        bits = lax.bitcast_convert_type(sc, i32)
        key = bits ^ ((bits >> 31) & jnp.int32(0x7FFFFFFF))
        if is_current:
            key = jnp.where((rowi < CHUNK) & (coli >= CHUNK), jnp.int32(INT_MIN), key)
        skey_ref[kb] = key

    def idx_loop(kb, c):
        index_block(kb, False)
        return c
    lax.fori_loop(0, j, idx_loop, 0)
    index_block(j, True)

    def count_where(pred_fn):
        def body(kb, acc):
            return acc + jnp.where(pred_fn(kb, skey_ref[kb]), 1.0, 0.0)
        acc = lax.fori_loop(0, j + 1, body, jnp.zeros((QB, QB), f32))
        return jnp.sum(acc, axis=1, keepdims=True)

    def bcast(col):
        return jnp.broadcast_to(col, (QB, QB))

    thr_ref[...] = jnp.full((QB, 1), INT_MIN, i32)
    ilim_ref[...] = jnp.full((QB, 1), S, i32)
    kf = float(n_sel)

    @pl.when((j + 1) * QB > n_sel)
    def _():
        zero = jnp.zeros((QB, 1), i32)
        c0 = count_where(lambda kb, key: key >= 0)
        n_all = jnp.zeros((QB, 1), f32) + ((j + 1) * QB).astype(f32)
        thr = jnp.where(c0 >= kf, zero, jnp.int32(INT_MIN))
        c_thr = jnp.where(c0 >= kf, c0, n_all)

        def bit_step(i, carry):
            thr, c_thr = carry
            cand = thr | jnp.left_shift(jnp.int32(1), 30 - i)
            cb = bcast(cand)
            c = count_where(lambda kb, key: key >= cb)
            ok = c >= kf
            return jnp.where(ok, cand, thr), jnp.where(ok, c, c_thr)
        thr, c_thr = lax.fori_loop(0, 31, bit_step, (thr, c_thr))
        thr_ref[...] = thr

        @pl.when(jnp.max(c_thr) > kf)
        def _():
            tb = bcast(thr)
            need = kf - count_where(lambda kb, key: key > tb)
            n_bits = int(math.ceil(math.log2(S + 1)))

            def lim_step(i, lim):
                cand = lim + jnp.left_shift(jnp.int32(1), n_bits - 1 - i)
                cb = bcast(cand)
                c = count_where(lambda kb, key: (key == tb) & (kb * QB + coli < cb))
                return jnp.where(c <= need, cand, lim)
            ilim_ref[...] = lax.fori_loop(0, n_bits, lim_step, zero)

    m_ref[...] = jnp.full(m_ref.shape, NEG_BIG, f32)
    l_ref[...] = jnp.zeros(l_ref.shape, f32)
    acc_ref[...] = jnp.zeros(acc_ref.shape, f32)
    thr_b = bcast(thr_ref[...])
    lim_b = bcast(ilim_ref[...])

    def attend(kb, bias_of_head):
        sl = pl.ds(pl.multiple_of(kb * QB, QB), QB)
        key = skey_ref[kb]
        sel = ((key > thr_b) | ((key == thr_b) & (kb * QB + coli < lim_b))) & (key > jnp.int32(INT_MIN))
        cblk = ckvn_ref[sl, :]
        for h in range(A_HEADS):
            s = lax.dot_general(qlat_ref[h], cblk, _NT, preferred_element_type=f32) * scale + bias_of_head(h)
            s = jnp.where(sel, s, NEG_BIG)
            m_old = m_ref[h]
            m_new = jnp.maximum(m_old, jnp.max(s, axis=1, keepdims=True))
            p = jnp.exp(s - m_new)
            alpha = jnp.exp(m_old - m_new)
            l_ref[h] = alpha * l_ref[h] + jnp.sum(p, axis=1, keepdims=True)
            acc_ref[h] = alpha * acc_ref[h] + jnp.dot(p.astype(bf16), cblk, preferred_element_type=f32)
            m_ref[h] = m_new

    def far_loop(kb, c):
        attend(kb, lambda h: bfar_ref[h:h + 1, :])
        return c
    lax.fori_loop(0, jnp.maximum(j - 1, 0), far_loop, 0)

    @pl.when(j >= 1)
    def _():
        attend(j - 1, lambda h: bt_ref[h, :, 0:QB])
    attend(j, lambda h: bt_ref[h, :, QB:2 * QB])

    outs = []
    for h in range(A_HEADS):
        o_lat = acc_ref[h] / l_ref[h]
        outs.append(jnp.dot(o_lat.astype(bf16), wuv_ref[h], preferred_element_type=f32))
    o_ref[...] = jnp.concatenate(outs, axis=1)


def _t5_bucket(rel):
    half = REL_BUCKETS // 2
    max_exact = half // 2
    base = jnp.where(rel > 0, half, 0)
    n = jnp.abs(rel)
    nf = jnp.maximum(n, 1).astype(f32)
    large = max_exact + (jnp.log(nf / max_exact) / math.log(REL_MAX_DIST / max_exact) * (half - max_exact)).astype(i32)
    large = jnp.minimum(large, half - 1)
    return base + jnp.where(n < max_exact, n, large)


def _dsa(proj, kv_norm_g, w_uk, w_uv, rel_bias, B, S):
    QB = Q_BLOCK
    nq = S // QB
    n_sel = min(TOPK_MAX, S // 4)
    assert REL_MAX_DIST <= QB
    r = jnp.arange(QB, dtype=i32)[:, None]
    c = jnp.arange(2 * QB, dtype=i32)[None, :]
    bias_tile = jnp.moveaxis(rel_bias[_t5_bucket(c - QB - r)], -1, 0).astype(f32)
    bias_far = jnp.broadcast_to(rel_bias[_t5_bucket(jnp.int32(-QB - 1))][:, None], (A_HEADS, LANES)).astype(f32)
    wuk_t = jnp.transpose(w_uk, (1, 2, 0)).astype(bf16)
    wuv_h = jnp.transpose(w_uv, (1, 0, 2)).astype(bf16)
    qblk = lambda width, cb: pl.BlockSpec((QB, width), lambda b, j, cb=cb: (b * nq + j, cb))
    seq = lambda cb: pl.BlockSpec((S, LANES), lambda b, j, cb=cb: (b, cb))
    full = lambda a: pl.BlockSpec(a.shape, lambda b, j, n=a.ndim: (0,) * n)
    return pl.pallas_call(
        functools.partial(_dsa_body, S=S, n_sel=n_sel),
        grid=(B, nq),
        in_specs=[qblk(A_Q, COL_QA // A_Q), qblk(A_Q, COL_QI // A_Q), qblk(LANES, COL_KIWI // LANES),
                  seq(COL_CKV // LANES), seq(COL_KIWI // LANES),
                  full(kv_norm_g), full(wuk_t), full(wuv_h), full(bias_tile), full(bias_far)],
        out_specs=pl.BlockSpec((QB, A_Q), lambda b, j: (b * nq + j, 0)),
        out_shape=jax.ShapeDtypeStruct((B * S, A_Q), f32),
        scratch_shapes=[
            pltpu.VMEM((S, A_LATENT), bf16),
            pltpu.VMEM((S // QB, QB, QB), i32),
            pltpu.VMEM((A_HEADS, QB, 1), f32),
            pltpu.VMEM((A_HEADS, QB, 1), f32),
            pltpu.VMEM((A_HEADS, QB, A_LATENT), f32),
            pltpu.VMEM((A_HEADS, QB, A_LATENT), bf16),
            pltpu.VMEM((IDX_HEADS, QB, IDX_DIM), bf16),
            pltpu.VMEM((IDX_HEADS, QB, QB), f32),
            pltpu.VMEM((QB, 1), i32),
            pltpu.VMEM((QB, 1), i32),
        ],
        compiler_params=_params("arbitrary", "arbitrary"),
        name="dsa",
    )(proj, proj, proj, proj, proj, kv_norm_g, wuk_t, wuv_h, bias_tile, bias_far)


def _layer_norm(y, g, b):
    mu = jnp.mean(y, axis=-1, keepdims=True)
    var = jnp.mean(jnp.square(y - mu), axis=-1, keepdims=True)
    return (y - mu) * lax.rsqrt(var + LN_EPS) * g + b


def _merge_body(oa_ref, ob_ref, ga_ref, gb_ref, x_ref, wa_ref, wb_ref, wo_ref, g_ref, b_ref, o_ref):
    ya = jnp.dot(oa_ref[...].astype(bf16), wa_ref[...], preferred_element_type=f32)
    yb = jnp.dot(ob_ref[...].astype(bf16), wb_ref[...], preferred_element_type=f32)
    merged = jax.nn.sigmoid(ga_ref[...]) * ya + jax.nn.sigmoid(gb_ref[...]) * yb
    mix = jnp.dot(merged.astype(bf16), wo_ref[...], preferred_element_type=f32)
    o_ref[...] = _layer_norm(ALPHA * x_ref[...] + mix, g_ref[...], b_ref[...])


def _merge(oa, ob, proj, x2, wa, wb, wo, g, b, tm=256):
    T = x2.shape[0]
    full = lambda a: pl.BlockSpec(a.shape, lambda i: (0, 0))
    return pl.pallas_call(
        _merge_body,
        grid=(T // tm,),
        in_specs=[pl.BlockSpec((tm, A_Q), lambda i: (i, 0)), pl.BlockSpec((tm, B_V), lambda i: (i, 0)),
                  pl.BlockSpec((tm, D_MODEL), lambda i: (i, COL_GATE_A // D_MODEL)),
                  pl.BlockSpec((tm, D_MODEL), lambda i: (i, COL_GATE_B // D_MODEL)),
                  pl.BlockSpec((tm, D_MODEL), lambda i: (i, 0)),
                  full(wa), full(wb), full(wo), full(g), full(b)],
        out_specs=pl.BlockSpec((tm, D_MODEL), lambda i: (i, 0)),
        out_shape=jax.ShapeDtypeStruct((T, D_MODEL), f32),
        compiler_params=_params("arbitrary"),
        name="merge",
    )(oa, ob, proj, proj, x2, wa, wb, wo, g, b)


def _route_body(x_ref, wpq_ref, sk1_ref, sk2_ref, ids_ref, gate_ref,
                v1_ref, i1_ref, v2_ref, i2_ref, ts_ref, ti_ref, idt_ref, gt_ref):
    tb = x_ref.shape[0]
    K = PEER_TOPK
    q = jnp.dot(x_ref[...].astype(bf16), wpq_ref[...], preferred_element_type=f32)
    kiota = lax.broadcasted_iota(i32, (PEER_KEYS, tb), 0)
    r8 = lax.broadcasted_iota(i32, (8, tb), 0)
    r16 = lax.broadcasted_iota(i32, (16, tb), 0)
    flat = jnp.concatenate([r16] + [K * a + r8 for a in range(1, 8)] + [K * (8 + r8)], axis=0)
    half = PEER_QDIM // 2
    for h in range(PEER_HEADS):
        for part, (sk_ref, v_ref, i_ref) in enumerate(((sk1_ref, v1_ref, i1_ref), (sk2_ref, v2_ref, i2_ref))):
            off = h * PEER_QDIM + part * half
            qh = q[:, off:off + half].astype(bf16)
            s = lax.dot_general(sk_ref[...], qh, _NT, preferred_element_type=f32)
            for r in range(K):
                m = jnp.max(s, axis=0, keepdims=True)
                ix = jnp.min(jnp.where(s == m, kiota, PEER_KEYS), axis=0, keepdims=True)
                v_ref[r:r + 1, :] = m
                i_ref[r:r + 1, :] = ix
                s = jnp.where(kiota == ix, -jnp.inf, s)
        v1, i1, v2, i2 = v1_ref[...], i1_ref[...], v2_ref[...], i2_ref[...]
        cand = [v1[0:1] + v2, ] + [v1[a:a + 1] + v2[0:8] for a in range(1, 8)] + [v1[8:16] + v2[0:1]]
        cid = [i1[0:1] * PEER_KEYS + i2, ] + [i1[a:a + 1] * PEER_KEYS + i2[0:8] for a in range(1, 8)] \
            + [i1[8:16] * PEER_KEYS + i2[0:1]]
        cand = jnp.concatenate(cand, axis=0)
        cid = jnp.concatenate(cid, axis=0)
        for r in range(K):
            m = jnp.max(cand, axis=0, keepdims=True)
            fsel = jnp.min(jnp.where(cand == m, flat, K * K), axis=0, keepdims=True)
            hit = flat == fsel
            ts_ref[r:r + 1, :] = m
            ti_ref[r:r + 1, :] = jnp.sum(jnp.where(hit, cid, 0), axis=0, keepdims=True)
            cand = jnp.where(hit, -jnp.inf, cand)
        ts = ts_ref[...]
        e = jnp.exp(ts - jnp.max(ts, axis=0, keepdims=True))
        gt_ref[h * K:(h + 1) * K, :] = e / jnp.sum(e, axis=0, keepdims=True)
        idt_ref[h * K:(h + 1) * K, :] = ti_ref[...]
    gate_ref[...] = gt_ref[...].T
    ids_ref[...] = lax.bitcast_convert_type(lax.bitcast_convert_type(idt_ref[...], f32).T, i32)


def _route(x1, wpq, sk1, sk2, tb=256):
    T = x1.shape[0]
    HK = PEER_HEADS * PEER_TOPK
    full = lambda a: pl.BlockSpec(a.shape, lambda i: (0, 0))
    return pl.pallas_call(
        _route_body,
        grid=(T // tb,),
        in_specs=[pl.BlockSpec((tb, D_MODEL), lambda i: (i, 0)), full(wpq), full(sk1), full(sk2)],
        out_specs=[pl.BlockSpec((tb, HK), lambda i: (i, 0)), pl.BlockSpec((tb, HK), lambda i: (i, 0))],
        out_shape=[jax.ShapeDtypeStruct((T, HK), i32), jax.ShapeDtypeStruct((T, HK), f32)],
        scratch_shapes=[pltpu.VMEM((PEER_TOPK, tb), f32), pltpu.VMEM((PEER_TOPK, tb), i32),
                        pltpu.VMEM((PEER_TOPK, tb), f32), pltpu.VMEM((PEER_TOPK, tb), i32),
                        pltpu.VMEM((PEER_TOPK, tb), f32), pltpu.VMEM((PEER_TOPK, tb), i32),
                        pltpu.VMEM((HK, tb), i32), pltpu.VMEM((HK, tb), f32)],
        compiler_params=_params("arbitrary"),
        name="route",
    )(x1, wpq, sk1, sk2)


def _pack_table(t):
    E, D = t.shape
    tb = t.astype(jnp.bfloat16)
    lo = lax.bitcast_convert_type(tb[:, :D // 2], jnp.uint16).astype(jnp.uint32)
    hi = lax.bitcast_convert_type(tb[:, D // 2:], jnp.uint16).astype(jnp.uint32)
    return lax.bitcast_convert_type((hi << 16) | lo, i32).reshape(E * ROWS_PER_EXPERT, LANES)


def _load_table_once(tab_hbm, tab_vmem, sem):
    @pl.when(pl.program_id(0) == 0)
    def _():
        cp = pltpu.make_async_copy(tab_hbm, tab_vmem, sem)
        cp.start()
        cp.wait()


def _gather_rows(ids_ref, t, tab_vmem, g_scr):
    R = ROWS_PER_EXPERT
    for k in range(PEER_HEADS * PEER_TOPK):
        e = ids_ref[t, k]
        g_scr[R * k:R * (k + 1), :] = tab_vmem[pl.ds(pl.multiple_of(e * R, R), R), :]


def _unpack(g):
    lo = lax.bitcast_convert_type(jnp.left_shift(g, 16), f32)
    hi = lax.bitcast_convert_type(g & jnp.int32(-65536), f32)
    return lo, hi


def _peer_u_body(ids_ref, x_ref, gate_ref, tab_hbm, foldt_ref, w_ref, tab_vmem, g_scr, h_scr, sem):
    _load_table_once(tab_hbm, tab_vmem, sem)
    tb = x_ref.shape[0]
    n_rows = g_scr.shape[0]
    ones8 = jnp.ones((8, LANES), f32)

    def tok(t, c):
        _gather_rows(ids_ref, t, tab_vmem, g_scr)
        x8 = x_ref[t]
        xlo = jnp.concatenate([x8[0:4], x8[0:4]], axis=0)
        xhi = jnp.concatenate([x8[4:8], x8[4:8]], axis=0)
        lo, hi = _unpack(g_scr[...])
        p = lo.reshape(n_rows // 8, 8, LANES) * xlo[None] + hi.reshape(n_rows // 8, 8, LANES) * xhi[None]
        p = p.reshape(n_rows, LANES)
        rs = lax.dot_general(ones8, p, _NT, precision=HIGHEST, preferred_element_type=f32)
        h8 = jnp.dot(rs, foldt_ref[...], precision=HIGHEST, preferred_element_type=f32)
        h_scr[pl.ds(t, 1), :] = h8[0:1]
        return c
    lax.fori_loop(0, tb, tok, 0)
    h = h_scr[...]
    w_ref[...] = gate_ref[...] * (0.5 * h * (1.0 + lax.erf(h * (2.0 ** -0.5))))


def _peer_u(ids, x1r, gate, tab, tb=128):
    T = ids.shape[0]
    HK = PEER_HEADS * PEER_TOPK
    n_rows = HK * ROWS_PER_EXPERT
    fold_t = (np.arange(n_rows)[:, None] // ROWS_PER_EXPERT == np.arange(HK)[None, :]).astype(np.float32)
    return pl.pallas_call(
        _peer_u_body,
        grid=(T // tb,),
        in_specs=[pl.BlockSpec((tb, HK), lambda i: (i, 0), memory_space=pltpu.SMEM),
                  pl.BlockSpec((tb, 8, LANES), lambda i: (i, 0, 0)),
                  pl.BlockSpec((tb, HK), lambda i: (i, 0)),
                  pl.BlockSpec(memory_space=pl.ANY),
                  pl.BlockSpec((n_rows, HK), lambda i: (0, 0))],
        out_specs=pl.BlockSpec((tb, HK), lambda i: (i, 0)),
        out_shape=jax.ShapeDtypeStruct((T, HK), f32),
        scratch_shapes=[pltpu.VMEM(tab.shape, i32), pltpu.VMEM((n_rows, LANES), i32),
                        pltpu.VMEM((tb, HK), f32), pltpu.SemaphoreType.DMA],
        compiler_params=_params("arbitrary", vmem=TABLE_VMEM_LIMIT),
        name="peer_u",
    )(ids, x1r, gate, tab, jnp.asarray(fold_t))


def _peer_v_body(ids_ref, w_ref, x_ref, tab_hbm, exp_ref, mask_ref, g_ref, b_ref, o_ref,
                 tab_vmem, g_scr, wexp_scr, sem):
    _load_table_once(tab_hbm, tab_vmem, sem)
    tb = x_ref.shape[0]
    wexp_scr[...] = jnp.dot(w_ref[...], exp_ref[...], precision=HIGHEST, preferred_element_type=f32)
    n_el = float(D_MODEL)

    def tok(t, c):
        _gather_rows(ids_ref, t, tab_vmem, g_scr)
        lo, hi = _unpack(g_scr[...])
        lhs = mask_ref[...] * wexp_scr[pl.ds(t, 1), :]
        o_lo = jnp.dot(lhs, lo, precision=HIGHEST, preferred_element_type=f32)
        o_hi = jnp.dot(lhs, hi, precision=HIGHEST, preferred_element_type=f32)
        ffn = jnp.concatenate([o_lo[0:4], o_hi[0:4]], axis=0)
        y = ALPHA * x_ref[t] + ffn
        mu = jnp.sum(y, axis=(0, 1), keepdims=True) / n_el
        yc = y - mu
        var = jnp.sum(yc * yc, axis=(0, 1), keepdims=True) / n_el
        o_ref[t] = yc * lax.rsqrt(var + LN_EPS) * g_ref[...] + b_ref[...]
        return c
    lax.fori_loop(0, tb, tok, 0)


def _peer_v(ids, w, x1r, tab, g, b, tb=128):
    T = ids.shape[0]
    HK = PEER_HEADS * PEER_TOPK
    R = ROWS_PER_EXPERT
    n_rows = HK * R
    expand = (np.arange(HK)[:, None] == np.arange(n_rows)[None, :] // R).astype(np.float32)
    mask = ((np.arange(n_rows)[None, :] % R) == np.arange(8)[:, None]).astype(np.float32)
    return pl.pallas_call(
        _peer_v_body,
        grid=(T // tb,),
        in_specs=[pl.BlockSpec((tb, HK), lambda i: (i, 0), memory_space=pltpu.SMEM),
                  pl.BlockSpec((tb, HK), lambda i: (i, 0)),
                  pl.BlockSpec((tb, 8, LANES), lambda i: (i, 0, 0)),
                  pl.BlockSpec(memory_space=pl.ANY),
                  pl.BlockSpec((HK, n_rows), lambda i: (0, 0)),
                  pl.BlockSpec((8, n_rows), lambda i: (0, 0)),
                  pl.BlockSpec((8, LANES), lambda i: (0, 0)),
                  pl.BlockSpec((8, LANES), lambda i: (0, 0))],
        out_specs=pl.BlockSpec((tb, 8, LANES), lambda i: (i, 0, 0)),
        out_shape=jax.ShapeDtypeStruct((T, 8, LANES), f32),
        scratch_shapes=[pltpu.VMEM(tab.shape, i32), pltpu.VMEM((n_rows, LANES), i32),
                        pltpu.VMEM((tb, n_rows), f32), pltpu.SemaphoreType.DMA],
        compiler_params=_params("arbitrary", vmem=TABLE_VMEM_LIMIT),
        name="peer_v",
    )(ids, w, x1r, tab, jnp.asarray(expand), jnp.asarray(mask), g, b)


def _regroup_w_in(w):
    offs = np.cumsum(np.array(IN_SPLITS))[:-1].tolist()
    qa, ckv, qi, ki, wi, qb, fb, ib, gb, gate_a, gate_b = jnp.split(w, offs, axis=-1)
    pad = jnp.zeros((w.shape[0], LANES - IDX_DIM - IDX_HEADS), w.dtype)
    out = jnp.concatenate([qa, qi, qb, fb, ib, gb, gate_a, gate_b, ckv, ki, wi, pad], axis=-1)
    assert out.shape[-1] == PROJ_COLS
    return out.astype(bf16)


def kernel(x, w_in, kv_norm_g, w_uk, w_uv, rel_bias, lb_params, b_norm_g, w_br_a, w_br_b, w_o, ln1_g, ln1_b,
           w_pq, sub_keys1, sub_keys2, u_table, v_table, ln2_g, ln2_b):
    B, S, D = x.shape
    T = B * S
    lower_bounds = jnp.cumsum(jax.nn.softmax(lb_params.astype(f32), axis=0), axis=0)
    x2 = x.reshape(T, D)
    for l in range(DEPTH):
        proj = _project(x2, _regroup_w_in(w_in[l]))
        oa = _dsa(proj, kv_norm_g[l][None, :], w_uk[l], w_uv[l], rel_bias, B, S)
        ob = _hgrn(proj, lower_bounds[l][None, :], b_norm_g[l][None, :], B, S)
        x1 = _merge(oa, ob, proj, x2, w_br_a[l].astype(bf16), w_br_b[l].astype(bf16), w_o[l].astype(bf16),
                    ln1_g[l][None, :], ln1_b[l][None, :])
        ids, gate = _route(x1, w_pq[l].astype(bf16), sub_keys1[l].astype(bf16), sub_keys2[l].astype(bf16))
        x1r = x1.reshape(T, 8, LANES)
        w = _peer_u(ids, x1r, gate, _pack_table(u_table[l]))
        x2 = _peer_v(ids, w, x1r, _pack_table(v_table[l]), ln2_g[l].reshape(8, LANES),
                     ln2_b[l].reshape(8, LANES)).reshape(T, D)
    return x2.reshape(B, S, D)
```

```python
import functools
import math

import numpy as np
import jax
import jax.numpy as jnp
from jax import lax
from jax.experimental import pallas as pl
from jax.experimental.pallas import tpu as pltpu

f32 = jnp.float32
bf16 = jnp.bfloat16
i32 = jnp.int32
HIGHEST = lax.Precision.HIGHEST

D_MODEL = 1024
DEPTH = 1
CHUNK = 64
Q_BLOCK = 128
A_HEADS = 8
A_HEAD_DIM = 64
A_Q = A_HEADS * A_HEAD_DIM
A_LATENT = 128
IDX_HEADS = 8
IDX_DIM = 64
TOPK_MAX = 256
REL_BUCKETS = 32
REL_MAX_DIST = 128
B_HEADS = 4
B_KEY_DIM = 128
B_VAL_DIM = 128
B_K = B_HEADS * B_KEY_DIM
B_V = B_HEADS * B_VAL_DIM
PEER_HEADS = 8
PEER_KEYS = 128
PEER_QDIM = 256
PEER_TOPK = 16
ALPHA = (2 * DEPTH) ** 0.25
LN_EPS = 1e-5
RMS_EPS = 1e-6

IN_SPLITS = (A_Q, A_LATENT, IDX_HEADS * IDX_DIM, IDX_DIM, IDX_HEADS, B_K, B_K, B_V, B_V, D_MODEL, D_MODEL)

COL_QA, COL_QI, COL_QB, COL_FB, COL_IB, COL_GB = 0, 512, 1024, 1536, 2048, 2560
COL_GATE_A, COL_GATE_B, COL_CKV, COL_KIWI = 3072, 4096, 5120, 5248
PROJ_COLS = 5376
LANES = 128
ROWS_PER_EXPERT = 4

INT_MIN = -(2 ** 31)
NEG_BIG = -1e30
VMEM_LIMIT = 48 * 1024 * 1024
TABLE_VMEM_LIMIT = 56 * 1024 * 1024

_NT = (((1,), (1,)), ((), ()))


def _params(*sem, vmem=VMEM_LIMIT):
    return pltpu.CompilerParams(dimension_semantics=sem, vmem_limit_bytes=vmem)


def _proj_body(x_ref, w_ref, o_ref):
    o_ref[...] = jnp.dot(x_ref[...].astype(bf16), w_ref[...], preferred_element_type=f32)


def _project(x2, w, tm=256):
    T = x2.shape[0]
    return pl.pallas_call(
        _proj_body,
        grid=(T // tm,),
        in_specs=[pl.BlockSpec((tm, D_MODEL), lambda i: (i, 0)),
                  pl.BlockSpec((D_MODEL, PROJ_COLS), lambda i: (0, 0))],
        out_specs=pl.BlockSpec((tm, PROJ_COLS), lambda i: (i, 0)),
        out_shape=jax.ShapeDtypeStruct((T, PROJ_COLS), f32),
        compiler_params=_params("arbitrary"),
        name="project",
    )(x2, w)


def _hgrn_body(q_ref, f_ref, i_ref, g_ref, lb_ref, ng_ref, o_ref, st_ref, *, n_chunks):
    @pl.when(pl.program_id(1) == 0)
    def _():
        st_ref[...] = jnp.zeros_like(st_ref)

    lb = lb_ref[...]
    ng = ng_ref[...]
    row = lax.broadcasted_iota(i32, (CHUNK, CHUNK), 0)
    col = lax.broadcasted_iota(i32, (CHUNK, CHUNK), 1)
    causal = row >= col
    tri = causal.astype(f32)
    for n in range(n_chunks):
        sl = pl.ds(n * CHUNK, CHUNK)
        z = f_ref[sl, :]
        log_f = jnp.log(lb + (1.0 - lb) * jax.nn.sigmoid(z))
        kk = (1.0 - lb) * jax.nn.sigmoid(-z)
        bc = jnp.dot(tri, log_f, precision=HIGHEST, preferred_element_type=f32)
        b_mid = bc[CHUNK // 2 - 1:CHUNK // 2, :]
        b_last = bc[CHUNK - 1:CHUNK, :]
        q = q_ref[sl, :]
        v = i_ref[sl, :]
        g = g_ref[sl, :]
        q_in = (q * jnp.exp(bc - b_mid)).astype(bf16)
        k_in = (kk * jnp.exp(b_mid - bc)).astype(bf16)
        k_dec = kk * jnp.exp(b_last - bc)
        q_dec = (q * jnp.exp(bc)).astype(bf16)
        decay = jnp.exp(b_last)
        outs = []
        for h in range(B_HEADS):
            hs = slice(h * B_KEY_DIM, (h + 1) * B_KEY_DIM)
            attn = lax.dot_general(q_in[:, hs], k_in[:, hs], _NT, preferred_element_type=f32)
            attn = jnp.where(causal, attn, 0.0)
            vh = v[:, hs]
            o = jnp.dot(attn.astype(bf16), vh.astype(bf16), preferred_element_type=f32)
            st = st_ref[h]
            o = o + lax.dot_general(q_dec[:, hs], st.astype(bf16), _NT, preferred_element_type=f32)
            d_st = jnp.dot(vh.T.astype(bf16), k_dec[:, hs].astype(bf16), preferred_element_type=f32)
            st_ref[h] = decay[:, hs] * st + d_st
            ms = jnp.mean(o * o, axis=-1, keepdims=True)
            gh = g[:, hs]
            outs.append(o * lax.rsqrt(ms + RMS_EPS) * ng * gh * jax.nn.sigmoid(gh))
        o_ref[sl, :] = jnp.concatenate(outs, axis=1)


def _hgrn(proj, lb, norm_g, B, S, tc=256):
    nc = S // tc
    blk = lambda c: pl.BlockSpec((tc, B_K), lambda b, s, c=c: (b * nc + s, c))
    return pl.pallas_call(
        functools.partial(_hgrn_body, n_chunks=tc // CHUNK),
        grid=(B, nc),
        in_specs=[blk(COL_QB // B_K), blk(COL_FB // B_K), blk(COL_IB // B_K), blk(COL_GB // B_K),
                  pl.BlockSpec((1, B_K), lambda b, s: (0, 0)),
                  pl.BlockSpec((1, B_VAL_DIM), lambda b, s: (0, 0))],
        out_specs=pl.BlockSpec((tc, B_V), lambda b, s: (b * nc + s, 0)),
        out_shape=jax.ShapeDtypeStruct((B * S, B_V), f32),
        scratch_shapes=[pltpu.VMEM((B_HEADS, B_VAL_DIM, B_KEY_DIM), f32)],
        compiler_params=_params("arbitrary", "arbitrary"),
        name="hgrn",
    )(proj, proj, proj, proj, lb, norm_g)


def _dsa_body(qa_ref, qi_ref, kwq_ref, ckv_ref, kw_ref, g_ref, wuk_ref, wuv_ref, bt_ref, bfar_ref, o_ref,
              ckvn_ref, skey_ref, s_ref, mt_ref, lt_ref, acc_ref, qlat_ref, qih_ref, wq_ref, thr_ref, ilim_ref,
              *, S, n_sel):
    j = pl.program_id(1)
    QB = Q_BLOCK
    n_kb = S // QB
    scale = A_HEAD_DIM ** -0.5
    rowi = lax.broadcasted_iota(i32, (QB, QB), 0)
    coli = lax.broadcasted_iota(i32, (QB, QB), 1)

    @pl.when(j == 0)
    def _():
        def norm_blk(kb, c):
            sl = pl.ds(pl.multiple_of(kb * QB, QB), QB)
            cb = ckv_ref[sl, :]
            ms = jnp.mean(cb * cb, axis=-1, keepdims=True)
            ckvn_ref[sl, :] = (cb * lax.rsqrt(ms + RMS_EPS) * g_ref[...]).astype(bf16)
            return c
        lax.fori_loop(0, n_kb, norm_blk, 0)

    qa = qa_ref[...]
    qi = qi_ref[...]
    wq = kwq_ref[:, IDX_DIM:IDX_DIM + IDX_HEADS] * ((IDX_HEADS * IDX_DIM) ** -0.5)
    for h in range(A_HEADS):
        qh = qa[:, h * A_HEAD_DIM:(h + 1) * A_HEAD_DIM].astype(bf16)
        qlat_ref[h * QB:(h + 1) * QB, :] = jnp.dot(qh, wuk_ref[h], preferred_element_type=f32).astype(bf16)
    for h in range(IDX_HEADS):
        qih_ref[h] = qi[:, h * IDX_DIM:(h + 1) * IDX_DIM].astype(bf16)
        wq_ref[h] = jnp.broadcast_to(wq[:, h:h + 1], (QB, QB))

    def index_block(kb, is_current):
        sl = pl.ds(pl.multiple_of(kb * QB, QB), QB)
        kblk = kw_ref[sl, :][:, :IDX_DIM].astype(bf16)
        sc = jnp.zeros((QB, QB), f32)
        for h in range(IDX_HEADS):
            lg = lax.dot_general(qih_ref[h], kblk, _NT, preferred_element_type=f32)
            sc = sc + wq_ref[h] * jnp.maximum(lg, 0.0)
        bits = lax.bitcast_convert_type(sc, i32)
        key = bits ^ ((bits >> 31) & jnp.int32(0x7FFFFFFF))
        if is_current:
            key = jnp.where((rowi < CHUNK) & (coli >= CHUNK), jnp.int32(INT_MIN), key)
        skey_ref[kb] = key

    def idx_loop(kb, c):
        index_block(kb, False)
        return c
    lax.fori_loop(0, j, idx_loop, 0)
    index_block(j, True)

    def count_where(pred_fn):
        def body(kb, acc):
            return acc + jnp.where(pred_fn(kb, skey_ref[kb]), 1.0, 0.0)
        acc = lax.fori_loop(0, j + 1, body, jnp.zeros((QB, QB), f32))
        return jnp.sum(acc, axis=1, keepdims=True)

    def bcast(col):
        return jnp.broadcast_to(col, (QB, QB))

    thr_ref[...] = jnp.full((QB, 1), INT_MIN, i32)
    ilim_ref[...] = jnp.full((QB, 1), S, i32)
    kf = float(n_sel)

    @pl.when((j + 1) * QB > n_sel)
    def _():
        zero = jnp.zeros((QB, 1), i32)
        c0 = count_where(lambda kb, key: key >= 0)
        n_all = jnp.zeros((QB, 1), f32) + ((j + 1) * QB).astype(f32)
        thr = jnp.where(c0 >= kf, zero, jnp.int32(INT_MIN))
        c_thr = jnp.where(c0 >= kf, c0, n_all)

        def bit_step(i, carry):
            thr, c_thr = carry
            cand = thr | jnp.left_shift(jnp.int32(1), 30 - i)
            cb = bcast(cand)
            c = count_where(lambda kb, key: key >= cb)
            ok = c >= kf
            return jnp.where(ok, cand, thr), jnp.where(ok, c, c_thr)
        thr, c_thr = lax.fori_loop(0, 31, bit_step, (thr, c_thr))
        thr_ref[...] = thr

        @pl.when(jnp.max(c_thr) > kf)
        def _():
            tb = bcast(thr)
            need = kf - count_where(lambda kb, key: key > tb)
            n_bits = int(math.ceil(math.log2(S + 1)))

            def lim_step(i, lim):
                cand = lim + jnp.left_shift(jnp.int32(1), n_bits - 1 - i)
                cb = bcast(cand)
                c = count_where(lambda kb, key: (key == tb) & (kb * QB + coli < cb))
                return jnp.where(c <= need, cand, lim)
            ilim_ref[...] = lax.fori_loop(0, n_bits, lim_step, zero)

    mt_ref[...] = jnp.full(mt_ref.shape, NEG_BIG, f32)
    thr_b = bcast(thr_ref[...])
    lim_b = bcast(ilim_ref[...])

    def score_block(kb, bias_of_head):
        sl = pl.ds(pl.multiple_of(kb * QB, QB), QB)
        key = skey_ref[kb]
        sel = ((key > thr_b) | ((key == thr_b) & (kb * QB + coli < lim_b))) & (key > jnp.int32(INT_MIN))
        s_all = lax.dot_general(qlat_ref[...], ckvn_ref[sl, :], _NT, preferred_element_type=f32)
        for h in range(A_HEADS):
            s = s_all[h * QB:(h + 1) * QB] * scale + bias_of_head(h)
            s = jnp.where(sel, s, NEG_BIG)
            s_ref[kb, h] = s
            mt_ref[h] = jnp.maximum(mt_ref[h], s)

    def far_loop(kb, c):
        score_block(kb, lambda h: bfar_ref[h:h + 1, :])
        return c
    lax.fori_loop(0, jnp.maximum(j - 1, 0), far_loop, 0)

    @pl.when(j >= 1)
    def _():
        score_block(j - 1, lambda h: bt_ref[h, :, 0:QB])
    score_block(j, lambda h: bt_ref[h, :, QB:2 * QB])

    for h in range(A_HEADS):
        mt_ref[h] = bcast(jnp.max(mt_ref[h], axis=1, keepdims=True))
    lt_ref[...] = jnp.zeros(lt_ref.shape, f32)
    acc_ref[...] = jnp.zeros(acc_ref.shape, f32)

    def pv_block(kb, c):
        sl = pl.ds(pl.multiple_of(kb * QB, QB), QB)
        ps = []
        for h in range(A_HEADS):
            p = jnp.exp(s_ref[kb, h] - mt_ref[h])
            lt_ref[h] = lt_ref[h] + p
            ps.append(p.astype(bf16))
        acc_ref[...] = acc_ref[...] + jnp.dot(jnp.concatenate(ps, axis=0), ckvn_ref[sl, :],
                                              preferred_element_type=f32)
        return c
    lax.fori_loop(0, j + 1, pv_block, 0)

    outs = []
    for h in range(A_HEADS):
        o_lat = acc_ref[h * QB:(h + 1) * QB, :] / jnp.sum(lt_ref[h], axis=1, keepdims=True)
        outs.append(jnp.dot(o_lat.astype(bf16), wuv_ref[h], preferred_element_type=f32))
    o_ref[...] = jnp.concatenate(outs, axis=1)


def _t5_bucket(rel):
    half = REL_BUCKETS // 2
    max_exact = half // 2
    base = jnp.where(rel > 0, half, 0)
    n = jnp.abs(rel)
    nf = jnp.maximum(n, 1).astype(f32)
    large = max_exact + (jnp.log(nf / max_exact) / math.log(REL_MAX_DIST / max_exact) * (half - max_exact)).astype(i32)
    large = jnp.minimum(large, half - 1)
    return base + jnp.where(n < max_exact, n, large)


def _dsa(proj, kv_norm_g, w_uk, w_uv, rel_bias, B, S):
    QB = Q_BLOCK
    nq = S // QB
    n_sel = min(TOPK_MAX, S // 4)
    assert REL_MAX_DIST <= QB
    r = jnp.arange(QB, dtype=i32)[:, None]
    c = jnp.arange(2 * QB, dtype=i32)[None, :]
    bias_tile = jnp.moveaxis(rel_bias[_t5_bucket(c - QB - r)], -1, 0).astype(f32)
    bias_far = jnp.broadcast_to(rel_bias[_t5_bucket(jnp.int32(-QB - 1))][:, None], (A_HEADS, LANES)).astype(f32)
    wuk_t = jnp.transpose(w_uk, (1, 2, 0)).astype(bf16)
    wuv_h = jnp.transpose(w_uv, (1, 0, 2)).astype(bf16)
    qblk = lambda width, cb: pl.BlockSpec((QB, width), lambda b, j, cb=cb: (b * nq + j, cb))
    seq = lambda cb: pl.BlockSpec((S, LANES), lambda b, j, cb=cb: (b, cb))
    full = lambda a: pl.BlockSpec(a.shape, lambda b, j, n=a.ndim: (0,) * n)
    return pl.pallas_call(
        functools.partial(_dsa_body, S=S, n_sel=n_sel),
        grid=(B, nq),
        in_specs=[qblk(A_Q, COL_QA // A_Q), qblk(A_Q, COL_QI // A_Q), qblk(LANES, COL_KIWI // LANES),
                  seq(COL_CKV // LANES), seq(COL_KIWI // LANES),
                  full(kv_norm_g), full(wuk_t), full(wuv_h), full(bias_tile), full(bias_far)],
        out_specs=pl.BlockSpec((QB, A_Q), lambda b, j: (b * nq + j, 0)),
        out_shape=jax.ShapeDtypeStruct((B * S, A_Q), f32),
        scratch_shapes=[
            pltpu.VMEM((S, A_LATENT), bf16),
            pltpu.VMEM((S // QB, QB, QB), i32),
            pltpu.VMEM((S // QB, A_HEADS, QB, QB), f32),
            pltpu.VMEM((A_HEADS, QB, QB), f32),
            pltpu.VMEM((A_HEADS, QB, QB), f32),
            pltpu.VMEM((A_HEADS * QB, A_LATENT), f32),
            pltpu.VMEM((A_HEADS * QB, A_LATENT), bf16),
            pltpu.VMEM((IDX_HEADS, QB, IDX_DIM), bf16),
            pltpu.VMEM((IDX_HEADS, QB, QB), f32),
            pltpu.VMEM((QB, 1), i32),
            pltpu.VMEM((QB, 1), i32),
        ],
        compiler_params=_params("arbitrary", "arbitrary"),
        name="dsa",
    )(proj, proj, proj, proj, proj, kv_norm_g, wuk_t, wuv_h, bias_tile, bias_far)


def _layer_norm(y, g, b):
    mu = jnp.mean(y, axis=-1, keepdims=True)
    var = jnp.mean(jnp.square(y - mu), axis=-1, keepdims=True)
    return (y - mu) * lax.rsqrt(var + LN_EPS) * g + b


def _merge_body(oa_ref, ob_ref, ga_ref, gb_ref, x_ref, wa_ref, wb_ref, wo_ref, g_ref, b_ref, o_ref):
    ya = jnp.dot(oa_ref[...].astype(bf16), wa_ref[...], preferred_element_type=f32)
    yb = jnp.dot(ob_ref[...].astype(bf16), wb_ref[...], preferred_element_type=f32)
    merged = jax.nn.sigmoid(ga_ref[...]) * ya + jax.nn.sigmoid(gb_ref[...]) * yb
    mix = jnp.dot(merged.astype(bf16), wo_ref[...], preferred_element_type=f32)
    o_ref[...] = _layer_norm(ALPHA * x_ref[...] + mix, g_ref[...], b_ref[...])


def _merge(oa, ob, proj, x2, wa, wb, wo, g, b, tm=256):
    T = x2.shape[0]
    full = lambda a: pl.BlockSpec(a.shape, lambda i: (0, 0))
    return pl.pallas_call(
        _merge_body,
        grid=(T // tm,),
        in_specs=[pl.BlockSpec((tm, A_Q), lambda i: (i, 0)), pl.BlockSpec((tm, B_V), lambda i: (i, 0)),
                  pl.BlockSpec((tm, D_MODEL), lambda i: (i, COL_GATE_A // D_MODEL)),
                  pl.BlockSpec((tm, D_MODEL), lambda i: (i, COL_GATE_B // D_MODEL)),
                  pl.BlockSpec((tm, D_MODEL), lambda i: (i, 0)),
                  full(wa), full(wb), full(wo), full(g), full(b)],
        out_specs=pl.BlockSpec((tm, D_MODEL), lambda i: (i, 0)),
        out_shape=jax.ShapeDtypeStruct((T, D_MODEL), f32),
        compiler_params=_params("arbitrary"),
        name="merge",
    )(oa, ob, proj, proj, x2, wa, wb, wo, g, b)


def _route_body(x_ref, wpq_ref, sk1_ref, sk2_ref, ids_ref, gate_ref,
                v1_ref, i1_ref, v2_ref, i2_ref, ts_ref, ti_ref, idt_ref, gt_ref):
    tb = x_ref.shape[0]
    K = PEER_TOPK
    q = jnp.dot(x_ref[...].astype(bf16), wpq_ref[...], preferred_element_type=f32)
    kiota = lax.broadcasted_iota(i32, (PEER_KEYS, tb), 0)
    r8 = lax.broadcasted_iota(i32, (8, tb), 0)
    r16 = lax.broadcasted_iota(i32, (16, tb), 0)
    flat = jnp.concatenate([r16] + [K * a + r8 for a in range(1, 8)] + [K * (8 + r8)], axis=0)
    half = PEER_QDIM // 2
    for h in range(PEER_HEADS):
        for part, (sk_ref, v_ref, i_ref) in enumerate(((sk1_ref, v1_ref, i1_ref), (sk2_ref, v2_ref, i2_ref))):
            off = h * PEER_QDIM + part * half
            qh = q[:, off:off + half].astype(bf16)
            s = lax.dot_general(sk_ref[...], qh, _NT, preferred_element_type=f32)
            for r in range(K):
                m = jnp.max(s, axis=0, keepdims=True)
                ix = jnp.min(jnp.where(s == m, kiota, PEER_KEYS), axis=0, keepdims=True)
                v_ref[r:r + 1, :] = m
                i_ref[r:r + 1, :] = ix
                s = jnp.where(kiota == ix, -jnp.inf, s)
        v1, i1, v2, i2 = v1_ref[...], i1_ref[...], v2_ref[...], i2_ref[...]
        cand = [v1[0:1] + v2, ] + [v1[a:a + 1] + v2[0:8] for a in range(1, 8)] + [v1[8:16] + v2[0:1]]
        cid = [i1[0:1] * PEER_KEYS + i2, ] + [i1[a:a + 1] * PEER_KEYS + i2[0:8] for a in range(1, 8)] \
            + [i1[8:16] * PEER_KEYS + i2[0:1]]
        cand = jnp.concatenate(cand, axis=0)
        cid = jnp.concatenate(cid, axis=0)
        for r in range(K):
            m = jnp.max(cand, axis=0, keepdims=True)
            fsel = jnp.min(jnp.where(cand == m, flat, K * K), axis=0, keepdims=True)
            hit = flat == fsel
            ts_ref[r:r + 1, :] = m
            ti_ref[r:r + 1, :] = jnp.sum(jnp.where(hit, cid, 0), axis=0, keepdims=True)
            cand = jnp.where(hit, -jnp.inf, cand)
        ts = ts_ref[...]
        e = jnp.exp(ts - jnp.max(ts, axis=0, keepdims=True))
        gt_ref[h * K:(h + 1) * K, :] = e / jnp.sum(e, axis=0, keepdims=True)
        idt_ref[h * K:(h + 1) * K, :] = ti_ref[...] * ROWS_PER_EXPERT
    gate_ref[...] = gt_ref[...].T
    ids_ref[...] = lax.bitcast_convert_type(lax.bitcast_convert_type(idt_ref[...], f32).T, i32)


def _route(x1, wpq, sk1, sk2, tb=256):
    T = x1.shape[0]
    HK = PEER_HEADS * PEER_TOPK
    full = lambda a: pl.BlockSpec(a.shape, lambda i: (0, 0))
    return pl.pallas_call(
        _route_body,
        grid=(T // tb,),
        in_specs=[pl.BlockSpec((tb, D_MODEL), lambda i: (i, 0)), full(wpq), full(sk1), full(sk2)],
        out_specs=[pl.BlockSpec((tb, HK), lambda i: (i, 0)), pl.BlockSpec((tb, HK), lambda i: (i, 0))],
        out_shape=[jax.ShapeDtypeStruct((T, HK), i32), jax.ShapeDtypeStruct((T, HK), f32)],
        scratch_shapes=[pltpu.VMEM((PEER_TOPK, tb), f32), pltpu.VMEM((PEER_TOPK, tb), i32),
                        pltpu.VMEM((PEER_TOPK, tb), f32), pltpu.VMEM((PEER_TOPK, tb), i32),
                        pltpu.VMEM((PEER_TOPK, tb), f32), pltpu.VMEM((PEER_TOPK, tb), i32),
                        pltpu.VMEM((HK, tb), i32), pltpu.VMEM((HK, tb), f32)],
        compiler_params=_params("arbitrary"),
        name="route",
    )(x1, wpq, sk1, sk2)


SLOTS = PEER_HEADS * PEER_TOPK
G_ROWS = SLOTS * ROWS_PER_EXPERT
G_ROWS16 = 2 * G_ROWS
TOKENS_PER_STEP = 2


def _pack_table(t):
    E, D = t.shape
    tb = lax.bitcast_convert_type(t.astype(jnp.bfloat16), jnp.uint16).astype(jnp.uint32)
    tb = tb.reshape(E, ROWS_PER_EXPERT, 2, LANES)
    return lax.bitcast_convert_type((tb[:, :, 1] << 16) | tb[:, :, 0], i32).reshape(E * ROWS_PER_EXPERT, LANES)


def _gather_rows(rows_ref, t, tab_ref, g_ref):
    R = ROWS_PER_EXPERT
    for k in range(SLOTS):
        r0 = rows_ref[t, k]
        g_ref[R * k:R * (k + 1), :] = tab_ref[pl.ds(pl.multiple_of(r0, R), R), :]


def _split2(a):
    hi = a.astype(bf16)
    return jnp.concatenate([hi, (a - hi.astype(f32)).astype(bf16)], axis=0)


def _token_pipeline(tb, rows_ref, tab_ref, g_scr, issue, finish):
    for u in range(TOKENS_PER_STEP):
        _gather_rows(rows_ref, u, tab_ref, g_scr.at[u])

    def step(i, c):
        for u in range(TOKENS_PER_STEP):
            t = i * TOKENS_PER_STEP + u
            z = issue(t, pltpu.bitcast(g_scr[u], jnp.bfloat16))
            _gather_rows(rows_ref, jnp.minimum(t + TOKENS_PER_STEP, tb - 1), tab_ref, g_scr.at[u])
            finish(t, z)
        return c
    lax.fori_loop(0, tb // TOKENS_PER_STEP, step, 0)


def _peer_u_body(rows_ref, x_ref, gate_ref, tab_ref, mask_ref, fold_ref, w_ref, g_scr, r_scr):
    tb = x_ref.shape[0]

    def issue(t, gb):
        return lax.dot_general(_split2(x_ref[t]), gb, _NT, preferred_element_type=f32)

    def finish(t, z):
        r_scr[pl.ds(t, 1), :] = jnp.sum((z[0:8] + z[8:16]) * mask_ref[...], axis=0, keepdims=True)

    _token_pipeline(tb, rows_ref, tab_ref, g_scr, issue, finish)
    h = jnp.dot(r_scr[...], fold_ref[...], precision=HIGHEST, preferred_element_type=f32)
    w_ref[...] = gate_ref[...] * (0.5 * h * (1.0 + lax.erf(h * (2.0 ** -0.5))))


def _chunk_mask():
    return (np.arange(G_ROWS16)[None, :] % 8 == np.arange(8)[:, None]).astype(np.float32)


def _peer_u(rows, x1r, gate, tab, tb=128):
    T = rows.shape[0]
    fold = (np.arange(G_ROWS16)[:, None] // 8 == np.arange(SLOTS)[None, :]).astype(np.float32)
    full = lambda a: pl.BlockSpec(a.shape, lambda i: (0, 0))
    return pl.pallas_call(
        _peer_u_body,
        grid=(T // tb,),
        in_specs=[pl.BlockSpec((tb, SLOTS), lambda i: (i, 0), memory_space=pltpu.SMEM),
                  pl.BlockSpec((tb, 8, LANES), lambda i: (i, 0, 0)),
                  pl.BlockSpec((tb, SLOTS), lambda i: (i, 0)),
                  full(tab), pl.BlockSpec((8, G_ROWS16), lambda i: (0, 0)),
                  pl.BlockSpec((G_ROWS16, SLOTS), lambda i: (0, 0))],
        out_specs=pl.BlockSpec((tb, SLOTS), lambda i: (i, 0)),
        out_shape=jax.ShapeDtypeStruct((T, SLOTS), f32),
        scratch_shapes=[pltpu.VMEM((TOKENS_PER_STEP, G_ROWS, LANES), i32), pltpu.VMEM((tb, G_ROWS16), f32)],
        compiler_params=_params("arbitrary", vmem=TABLE_VMEM_LIMIT),
        name="peer_u",
    )(rows, x1r, gate, tab, jnp.asarray(_chunk_mask()), jnp.asarray(fold))


def _peer_v_body(rows_ref, w_ref, x_ref, tab_ref, exp_ref, mask_ref, g_ref, b_ref, o_ref, g_scr, wexp_scr):
    tb = x_ref.shape[0]
    wexp_scr[...] = jnp.dot(w_ref[...], exp_ref[...], precision=HIGHEST, preferred_element_type=f32)

    def issue(t, gb):
        return jnp.dot(_split2(mask_ref[...] * wexp_scr[pl.ds(t, 1), :]), gb, preferred_element_type=f32)

    def finish(t, z):
        o_ref[t] = z[0:8] + z[8:16]

    _token_pipeline(tb, rows_ref, tab_ref, g_scr, issue, finish)
    y = ALPHA * x_ref[...] + o_ref[...]
    n_el = float(D_MODEL)
    mu = jnp.sum(y, axis=(1, 2), keepdims=True) / n_el
    yc = y - mu
    var = jnp.sum(yc * yc, axis=(1, 2), keepdims=True) / n_el
    o_ref[...] = yc * lax.rsqrt(var + LN_EPS) * g_ref[...] + b_ref[...]


def _peer_v(rows, w, x1r, tab, g, b, tb=128):
    T = rows.shape[0]
    expand = (np.arange(SLOTS)[:, None] == np.arange(G_ROWS16)[None, :] // 8).astype(np.float32)
    full = lambda a: pl.BlockSpec(a.shape, lambda i: (0, 0))
    return pl.pallas_call(
        _peer_v_body,
        grid=(T // tb,),
        in_specs=[pl.BlockSpec((tb, SLOTS), lambda i: (i, 0), memory_space=pltpu.SMEM),
                  pl.BlockSpec((tb, SLOTS), lambda i: (i, 0)),
                  pl.BlockSpec((tb, 8, LANES), lambda i: (i, 0, 0)),
                  full(tab), pl.BlockSpec((SLOTS, G_ROWS16), lambda i: (0, 0)),
                  pl.BlockSpec((8, G_ROWS16), lambda i: (0, 0)),
                  pl.BlockSpec((8, LANES), lambda i: (0, 0)),
                  pl.BlockSpec((8, LANES), lambda i: (0, 0))],
        out_specs=pl.BlockSpec((tb, 8, LANES), lambda i: (i, 0, 0)),
        out_shape=jax.ShapeDtypeStruct((T, 8, LANES), f32),
        scratch_shapes=[pltpu.VMEM((TOKENS_PER_STEP, G_ROWS, LANES), i32), pltpu.VMEM((tb, G_ROWS16), f32)],
        compiler_params=_params("arbitrary", vmem=TABLE_VMEM_LIMIT),
        name="peer_v",
    )(rows, w, x1r, tab, jnp.asarray(expand), jnp.asarray(_chunk_mask()), g, b)


def _regroup_w_in(w):
    offs = np.cumsum(np.array(IN_SPLITS))[:-1].tolist()
    qa, ckv, qi, ki, wi, qb, fb, ib, gb, gate_a, gate_b = jnp.split(w, offs, axis=-1)
    pad = jnp.zeros((w.shape[0], LANES - IDX_DIM - IDX_HEADS), w.dtype)
    out = jnp.concatenate([qa, qi, qb, fb, ib, gb, gate_a, gate_b, ckv, ki, wi, pad], axis=-1)
    assert out.shape[-1] == PROJ_COLS
    return out.astype(bf16)


def kernel(x, w_in, kv_norm_g, w_uk, w_uv, rel_bias, lb_params, b_norm_g, w_br_a, w_br_b, w_o, ln1_g, ln1_b,
           w_pq, sub_keys1, sub_keys2, u_table, v_table, ln2_g, ln2_b):
    B, S, D = x.shape
    T = B * S
    lower_bounds = jnp.cumsum(jax.nn.softmax(lb_params.astype(f32), axis=0), axis=0)
    x2 = x.reshape(T, D)
    for l in range(DEPTH):
        proj = _project(x2, _regroup_w_in(w_in[l]))
        oa = _dsa(proj, kv_norm_g[l][None, :], w_uk[l], w_uv[l], rel_bias, B, S)
        ob = _hgrn(proj, lower_bounds[l][None, :], b_norm_g[l][None, :], B, S)
        x1 = _merge(oa, ob, proj, x2, w_br_a[l].astype(bf16), w_br_b[l].astype(bf16), w_o[l].astype(bf16),
                    ln1_g[l][None, :], ln1_b[l][None, :])
        ids, gate = _route(x1, w_pq[l].astype(bf16), sub_keys1[l].astype(bf16), sub_keys2[l].astype(bf16))
        x1r = x1.reshape(T, 8, LANES)
        w = _peer_u(ids, x1r, gate, _pack_table(u_table[l]))
        x2 = _peer_v(ids, w, x1r, _pack_table(v_table[l]), ln2_g[l].reshape(8, LANES),
                     ln2_b[l].reshape(8, LANES)).reshape(T, D)
    return x2.reshape(B, S, D)
```

```python
import functools
import math

import numpy as np
import jax
import jax.numpy as jnp
from jax import lax
from jax.experimental import pallas as pl
from jax.experimental.pallas import tpu as pltpu

f32 = jnp.float32
bf16 = jnp.bfloat16
i32 = jnp.int32
HIGHEST = lax.Precision.HIGHEST

D_MODEL = 1024
DEPTH = 1
CHUNK = 64
Q_BLOCK = 128
A_HEADS = 8
A_HEAD_DIM = 64
A_Q = A_HEADS * A_HEAD_DIM
A_LATENT = 128
IDX_HEADS = 8
IDX_DIM = 64
TOPK_MAX = 256
REL_BUCKETS = 32
REL_MAX_DIST = 128
B_HEADS = 4
B_KEY_DIM = 128
B_VAL_DIM = 128
B_K = B_HEADS * B_KEY_DIM
B_V = B_HEADS * B_VAL_DIM
PEER_HEADS = 8
PEER_KEYS = 128
PEER_QDIM = 256
PEER_TOPK = 16
ALPHA = (2 * DEPTH) ** 0.25
LN_EPS = 1e-5
RMS_EPS = 1e-6

IN_SPLITS = (A_Q, A_LATENT, IDX_HEADS * IDX_DIM, IDX_DIM, IDX_HEADS, B_K, B_K, B_V, B_V, D_MODEL, D_MODEL)

COL_QA, COL_QI, COL_QB, COL_FB, COL_IB, COL_GB = 0, 512, 1024, 1536, 2048, 2560
COL_GATE_A, COL_GATE_B, COL_CKV, COL_KIWI = 3072, 4096, 5120, 5248
PROJ_COLS = 5376
LANES = 128
ROWS_PER_EXPERT = 4

INT_MIN = -(2 ** 31)
NEG_BIG = -1e30
VMEM_LIMIT = 48 * 1024 * 1024
TABLE_VMEM_LIMIT = 56 * 1024 * 1024

_NT = (((1,), (1,)), ((), ()))


def _params(*sem, vmem=VMEM_LIMIT):
    return pltpu.CompilerParams(dimension_semantics=sem, vmem_limit_bytes=vmem)


def _proj_body(x_ref, w_ref, o_ref):
    o_ref[...] = jnp.dot(x_ref[...].astype(bf16), w_ref[...], preferred_element_type=f32)


def _project(x2, w, tm=256):
    T = x2.shape[0]
    return pl.pallas_call(
        _proj_body,
        grid=(T // tm,),
        in_specs=[pl.BlockSpec((tm, D_MODEL), lambda i: (i, 0)),
                  pl.BlockSpec((D_MODEL, PROJ_COLS), lambda i: (0, 0))],
        out_specs=pl.BlockSpec((tm, PROJ_COLS), lambda i: (i, 0)),
        out_shape=jax.ShapeDtypeStruct((T, PROJ_COLS), f32),
        compiler_params=_params("arbitrary"),
        name="project",
    )(x2, w)


def _hgrn_body(q_ref, f_ref, i_ref, g_ref, lb_ref, ng_ref, o_ref, st_ref, *, n_chunks):
    @pl.when(pl.program_id(1) == 0)
    def _():
        st_ref[...] = jnp.zeros_like(st_ref)

    lb = lb_ref[...]
    ng = ng_ref[...]
    row = lax.broadcasted_iota(i32, (CHUNK, CHUNK), 0)
    col = lax.broadcasted_iota(i32, (CHUNK, CHUNK), 1)
    causal = row >= col
    tri = causal.astype(f32)
    for n in range(n_chunks):
        sl = pl.ds(n * CHUNK, CHUNK)
        z = f_ref[sl, :]
        log_f = jnp.log(lb + (1.0 - lb) * jax.nn.sigmoid(z))
        kk = (1.0 - lb) * jax.nn.sigmoid(-z)
        bc = jnp.dot(tri, log_f, precision=HIGHEST, preferred_element_type=f32)
        b_mid = bc[CHUNK // 2 - 1:CHUNK // 2, :]
        b_last = bc[CHUNK - 1:CHUNK, :]
        q = q_ref[sl, :]
        v = i_ref[sl, :]
        g = g_ref[sl, :]
        q_in = (q * jnp.exp(bc - b_mid)).astype(bf16)
        k_in = (kk * jnp.exp(b_mid - bc)).astype(bf16)
        k_dec = kk * jnp.exp(b_last - bc)
        q_dec = (q * jnp.exp(bc)).astype(bf16)
        decay = jnp.exp(b_last)
        outs = []
        for h in range(B_HEADS):
            hs = slice(h * B_KEY_DIM, (h + 1) * B_KEY_DIM)
            attn = lax.dot_general(q_in[:, hs], k_in[:, hs], _NT, preferred_element_type=f32)
            attn = jnp.where(causal, attn, 0.0)
            vh = v[:, hs]
            o = jnp.dot(attn.astype(bf16), vh.astype(bf16), preferred_element_type=f32)
            st = st_ref[h]
            o = o + lax.dot_general(q_dec[:, hs], st.astype(bf16), _NT, preferred_element_type=f32)
            d_st = jnp.dot(vh.T.astype(bf16), k_dec[:, hs].astype(bf16), preferred_element_type=f32)
            st_ref[h] = decay[:, hs] * st + d_st
            ms = jnp.mean(o * o, axis=-1, keepdims=True)
            gh = g[:, hs]
            outs.append(o * lax.rsqrt(ms + RMS_EPS) * ng * gh * jax.nn.sigmoid(gh))
        o_ref[sl, :] = jnp.concatenate(outs, axis=1)


def _hgrn(proj, lb, norm_g, B, S, tc=256):
    nc = S // tc
    blk = lambda c: pl.BlockSpec((tc, B_K), lambda b, s, c=c: (b * nc + s, c))
    return pl.pallas_call(
        functools.partial(_hgrn_body, n_chunks=tc // CHUNK),
        grid=(B, nc),
        in_specs=[blk(COL_QB // B_K), blk(COL_FB // B_K), blk(COL_IB // B_K), blk(COL_GB // B_K),
                  pl.BlockSpec((1, B_K), lambda b, s: (0, 0)),
                  pl.BlockSpec((1, B_VAL_DIM), lambda b, s: (0, 0))],
        out_specs=pl.BlockSpec((tc, B_V), lambda b, s: (b * nc + s, 0)),
        out_shape=jax.ShapeDtypeStruct((B * S, B_V), f32),
        scratch_shapes=[pltpu.VMEM((B_HEADS, B_VAL_DIM, B_KEY_DIM), f32)],
        compiler_params=_params("arbitrary", "arbitrary"),
        name="hgrn",
    )(proj, proj, proj, proj, lb, norm_g)


def _dsa_body(qa_ref, qi_ref, kwq_ref, ckv_ref, kw_ref, g_ref, wuk_ref, wuv_ref, bt_ref, bfar_ref, o_ref,
              ckvn_ref, skey_ref, s_ref, mt_ref, lt_ref, acc_ref, qlat_ref, qih_ref, wq_ref, thr_ref, ilim_ref,
              *, S, n_sel):
    j = pl.program_id(1)
    QB = Q_BLOCK
    n_kb = S // QB
    scale = A_HEAD_DIM ** -0.5
    rowi = lax.broadcasted_iota(i32, (QB, QB), 0)
    coli = lax.broadcasted_iota(i32, (QB, QB), 1)

    @pl.when(j == 0)
    def _():
        def norm_blk(kb, c):
            sl = pl.ds(pl.multiple_of(kb * QB, QB), QB)
            cb = ckv_ref[sl, :]
            ms = jnp.mean(cb * cb, axis=-1, keepdims=True)
            ckvn_ref[sl, :] = (cb * lax.rsqrt(ms + RMS_EPS) * g_ref[...]).astype(bf16)
            return c
        lax.fori_loop(0, n_kb, norm_blk, 0)

    qa = qa_ref[...]
    qi = qi_ref[...]
    wq = kwq_ref[:, IDX_DIM:IDX_DIM + IDX_HEADS] * ((IDX_HEADS * IDX_DIM) ** -0.5)
    for h in range(A_HEADS):
        qh = qa[:, h * A_HEAD_DIM:(h + 1) * A_HEAD_DIM].astype(bf16)
        qlat_ref[h * QB:(h + 1) * QB, :] = jnp.dot(qh, wuk_ref[h], preferred_element_type=f32).astype(bf16)
    for h in range(IDX_HEADS):
        qih_ref[h * QB:(h + 1) * QB, :] = qi[:, h * IDX_DIM:(h + 1) * IDX_DIM].astype(bf16)
        wq_ref[h] = jnp.broadcast_to(wq[:, h:h + 1], (QB, QB))

    def index_block(kb, is_current):
        sl = pl.ds(pl.multiple_of(kb * QB, QB), QB)
        kblk = kw_ref[sl, :][:, :IDX_DIM].astype(bf16)
        lg_all = lax.dot_general(qih_ref[...], kblk, _NT, preferred_element_type=f32)
        sc = jnp.zeros((QB, QB), f32)
        for h in range(IDX_HEADS):
            sc = sc + wq_ref[h] * jnp.maximum(lg_all[h * QB:(h + 1) * QB], 0.0)
        bits = lax.bitcast_convert_type(sc, i32)
        key = bits ^ ((bits >> 31) & jnp.int32(0x7FFFFFFF))
        if is_current:
            key = jnp.where((rowi < CHUNK) & (coli >= CHUNK), jnp.int32(INT_MIN), key)
        skey_ref[kb] = key

    def idx_loop(kb, c):
        index_block(kb, False)
        return c
    lax.fori_loop(0, j, idx_loop, 0)
    index_block(j, True)

    def count_where(pred_fn):
        def body(kb, acc):
            return acc + jnp.where(pred_fn(kb, skey_ref[kb]), 1.0, 0.0)
        acc = lax.fori_loop(0, j + 1, body, jnp.zeros((QB, QB), f32))
        return jnp.sum(acc, axis=1, keepdims=True)

    def bcast(col):
        return jnp.broadcast_to(col, (QB, QB))

    thr_ref[...] = jnp.full((QB, 1), INT_MIN, i32)
    ilim_ref[...] = jnp.full((QB, 1), S, i32)
    kf = float(n_sel)

    @pl.when((j + 1) * QB > n_sel)
    def _():
        zero = jnp.zeros((QB, 1), i32)
        c0 = count_where(lambda kb, key: key >= 0)
        n_all = jnp.zeros((QB, 1), f32) + ((j + 1) * QB).astype(f32)
        thr = jnp.where(c0 >= kf, zero, jnp.int32(INT_MIN))
        c_thr = jnp.where(c0 >= kf, c0, n_all)

        def bit_step(i, carry):
            thr, c_thr = carry
            cand = thr | jnp.left_shift(jnp.int32(1), 30 - i)
            cb = bcast(cand)
            c = count_where(lambda kb, key: key >= cb)
            ok = c >= kf
            return jnp.where(ok, cand, thr), jnp.where(ok, c, c_thr)
        thr, c_thr = lax.fori_loop(0, 31, bit_step, (thr, c_thr))
        thr_ref[...] = thr

        @pl.when(jnp.max(c_thr) > kf)
        def _():
            tb = bcast(thr)
            need = kf - count_where(lambda kb, key: key > tb)
            n_bits = int(math.ceil(math.log2(S + 1)))

            def lim_step(i, lim):
                cand = lim + jnp.left_shift(jnp.int32(1), n_bits - 1 - i)
                cb = bcast(cand)
                c = count_where(lambda kb, key: (key == tb) & (kb * QB + coli < cb))
                return jnp.where(c <= need, cand, lim)
            ilim_ref[...] = lax.fori_loop(0, n_bits, lim_step, zero)

    mt_ref[...] = jnp.full(mt_ref.shape, NEG_BIG, f32)
    thr_b = bcast(thr_ref[...])
    lim_b = bcast(ilim_ref[...])

    def score_block(kb, bias_of_head):
        sl = pl.ds(pl.multiple_of(kb * QB, QB), QB)
        key = skey_ref[kb]
        sel = ((key > thr_b) | ((key == thr_b) & (kb * QB + coli < lim_b))) & (key > jnp.int32(INT_MIN))
        s_all = lax.dot_general(qlat_ref[...], ckvn_ref[sl, :], _NT, preferred_element_type=f32)
        for h in range(A_HEADS):
            s = s_all[h * QB:(h + 1) * QB] * scale + bias_of_head(h)
            s = jnp.where(sel, s, NEG_BIG)
            s_ref[kb, h] = s
            mt_ref[h] = jnp.maximum(mt_ref[h], s)

    def far_loop(kb, c):
        score_block(kb, lambda h: bfar_ref[h:h + 1, :])
        return c
    lax.fori_loop(0, jnp.maximum(j - 1, 0), far_loop, 0)

    @pl.when(j >= 1)
    def _():
        score_block(j - 1, lambda h: bt_ref[h, :, 0:QB])
    score_block(j, lambda h: bt_ref[h, :, QB:2 * QB])

    for h in range(A_HEADS):
        mt_ref[h] = bcast(jnp.max(mt_ref[h], axis=1, keepdims=True))
    lt_ref[...] = jnp.zeros(lt_ref.shape, f32)
    acc_ref[...] = jnp.zeros(acc_ref.shape, f32)

    def pv_block(kb, c):
        sl = pl.ds(pl.multiple_of(kb * QB, QB), QB)
        ps = []
        for h in range(A_HEADS):
            p = jnp.exp(s_ref[kb, h] - mt_ref[h])
            lt_ref[h] = lt_ref[h] + p
            ps.append(p.astype(bf16))
        acc_ref[...] = acc_ref[...] + jnp.dot(jnp.concatenate(ps, axis=0), ckvn_ref[sl, :],
                                              preferred_element_type=f32)
        return c
    lax.fori_loop(0, j + 1, pv_block, 0)

    outs = []
    for h in range(A_HEADS):
        o_lat = acc_ref[h * QB:(h + 1) * QB, :] / jnp.sum(lt_ref[h], axis=1, keepdims=True)
        outs.append(jnp.dot(o_lat.astype(bf16), wuv_ref[h], preferred_element_type=f32))
    o_ref[...] = jnp.concatenate(outs, axis=1)


def _t5_bucket(rel):
    half = REL_BUCKETS // 2
    max_exact = half // 2
    base = jnp.where(rel > 0, half, 0)
    n = jnp.abs(rel)
    nf = jnp.maximum(n, 1).astype(f32)
    large = max_exact + (jnp.log(nf / max_exact) / math.log(REL_MAX_DIST / max_exact) * (half - max_exact)).astype(i32)
    large = jnp.minimum(large, half - 1)
    return base + jnp.where(n < max_exact, n, large)


def _dsa(proj, kv_norm_g, w_uk, w_uv, rel_bias, B, S):
    QB = Q_BLOCK
    nq = S // QB
    n_sel = min(TOPK_MAX, S // 4)
    assert REL_MAX_DIST <= QB
    r = jnp.arange(QB, dtype=i32)[:, None]
    c = jnp.arange(2 * QB, dtype=i32)[None, :]
    bias_tile = jnp.moveaxis(rel_bias[_t5_bucket(c - QB - r)], -1, 0).astype(f32)
    bias_far = jnp.broadcast_to(rel_bias[_t5_bucket(jnp.int32(-QB - 1))][:, None], (A_HEADS, LANES)).astype(f32)
    wuk_t = jnp.transpose(w_uk, (1, 2, 0)).astype(bf16)
    wuv_h = jnp.transpose(w_uv, (1, 0, 2)).astype(bf16)
    qblk = lambda width, cb: pl.BlockSpec((QB, width), lambda b, j, cb=cb: (b * nq + j, cb))
    seq = lambda cb: pl.BlockSpec((S, LANES), lambda b, j, cb=cb: (b, cb))
    full = lambda a: pl.BlockSpec(a.shape, lambda b, j, n=a.ndim: (0,) * n)
    return pl.pallas_call(
        functools.partial(_dsa_body, S=S, n_sel=n_sel),
        grid=(B, nq),
        in_specs=[qblk(A_Q, COL_QA // A_Q), qblk(A_Q, COL_QI // A_Q), qblk(LANES, COL_KIWI // LANES),
                  seq(COL_CKV // LANES), seq(COL_KIWI // LANES),
                  full(kv_norm_g), full(wuk_t), full(wuv_h), full(bias_tile), full(bias_far)],
        out_specs=pl.BlockSpec((QB, A_Q), lambda b, j: (b * nq + j, 0)),
        out_shape=jax.ShapeDtypeStruct((B * S, A_Q), f32),
        scratch_shapes=[
            pltpu.VMEM((S, A_LATENT), bf16),
            pltpu.VMEM((S // QB, QB, QB), i32),
            pltpu.VMEM((S // QB, A_HEADS, QB, QB), f32),
            pltpu.VMEM((A_HEADS, QB, QB), f32),
            pltpu.VMEM((A_HEADS, QB, QB), f32),
            pltpu.VMEM((A_HEADS * QB, A_LATENT), f32),
            pltpu.VMEM((A_HEADS * QB, A_LATENT), bf16),
            pltpu.VMEM((IDX_HEADS * QB, IDX_DIM), bf16),
            pltpu.VMEM((IDX_HEADS, QB, QB), f32),
            pltpu.VMEM((QB, 1), i32),
            pltpu.VMEM((QB, 1), i32),
        ],
        compiler_params=_params("arbitrary", "arbitrary"),
        name="dsa",
    )(proj, proj, proj, proj, proj, kv_norm_g, wuk_t, wuv_h, bias_tile, bias_far)


def _layer_norm(y, g, b):
    mu = jnp.mean(y, axis=-1, keepdims=True)
    var = jnp.mean(jnp.square(y - mu), axis=-1, keepdims=True)
    return (y - mu) * lax.rsqrt(var + LN_EPS) * g + b


def _merge_body(oa_ref, ob_ref, ga_ref, gb_ref, x_ref, wa_ref, wb_ref, wo_ref, g_ref, b_ref, o_ref):
    ya = jnp.dot(oa_ref[...].astype(bf16), wa_ref[...], preferred_element_type=f32)
    yb = jnp.dot(ob_ref[...].astype(bf16), wb_ref[...], preferred_element_type=f32)
    merged = jax.nn.sigmoid(ga_ref[...]) * ya + jax.nn.sigmoid(gb_ref[...]) * yb
    mix = jnp.dot(merged.astype(bf16), wo_ref[...], preferred_element_type=f32)
    o_ref[...] = _layer_norm(ALPHA * x_ref[...] + mix, g_ref[...], b_ref[...])


def _merge(oa, ob, proj, x2, wa, wb, wo, g, b, tm=256):
    T = x2.shape[0]
    full = lambda a: pl.BlockSpec(a.shape, lambda i: (0, 0))
    return pl.pallas_call(
        _merge_body,
        grid=(T // tm,),
        in_specs=[pl.BlockSpec((tm, A_Q), lambda i: (i, 0)), pl.BlockSpec((tm, B_V), lambda i: (i, 0)),
                  pl.BlockSpec((tm, D_MODEL), lambda i: (i, COL_GATE_A // D_MODEL)),
                  pl.BlockSpec((tm, D_MODEL), lambda i: (i, COL_GATE_B // D_MODEL)),
                  pl.BlockSpec((tm, D_MODEL), lambda i: (i, 0)),
                  full(wa), full(wb), full(wo), full(g), full(b)],
        out_specs=pl.BlockSpec((tm, D_MODEL), lambda i: (i, 0)),
        out_shape=jax.ShapeDtypeStruct((T, D_MODEL), f32),
        compiler_params=_params("arbitrary"),
        name="merge",
    )(oa, ob, proj, proj, x2, wa, wb, wo, g, b)


def _route_body(x_ref, wpq_ref, sk1_ref, sk2_ref, ids_ref, gate_ref,
                v1_ref, i1_ref, v2_ref, i2_ref, ts_ref, ti_ref, idt_ref, gt_ref):
    tb = x_ref.shape[0]
    K = PEER_TOPK
    q = jnp.dot(x_ref[...].astype(bf16), wpq_ref[...], preferred_element_type=f32)
    kiota = lax.broadcasted_iota(i32, (PEER_KEYS, tb), 0).astype(f32)
    r8 = lax.broadcasted_iota(i32, (8, tb), 0).astype(f32)
    r16 = lax.broadcasted_iota(i32, (16, tb), 0).astype(f32)
    flat = jnp.concatenate([r16] + [K * a + r8 for a in range(1, 8)] + [K * (8 + r8)], axis=0)
    half = PEER_QDIM // 2
    for h in range(PEER_HEADS):
        for part, (sk_ref, v_ref, i_ref) in enumerate(((sk1_ref, v1_ref, i1_ref), (sk2_ref, v2_ref, i2_ref))):
            off = h * PEER_QDIM + part * half
            qh = q[:, off:off + half].astype(bf16)
            s = lax.dot_general(sk_ref[...], qh, _NT, preferred_element_type=f32)
            for r in range(K):
                m = jnp.max(s, axis=0, keepdims=True)
                ix = jnp.min(jnp.where(s == m, kiota, float(PEER_KEYS)), axis=0, keepdims=True)
                v_ref[r:r + 1, :] = m
                i_ref[r:r + 1, :] = ix
                s = jnp.where(kiota == ix, -jnp.inf, s)
        v1, i1, v2, i2 = v1_ref[...], i1_ref[...], v2_ref[...], i2_ref[...]
        cand = [v1[0:1] + v2, ] + [v1[a:a + 1] + v2[0:8] for a in range(1, 8)] + [v1[8:16] + v2[0:1]]
        cid = [i1[0:1] * PEER_KEYS + i2, ] + [i1[a:a + 1] * PEER_KEYS + i2[0:8] for a in range(1, 8)] \
            + [i1[8:16] * PEER_KEYS + i2[0:1]]
        cand = jnp.concatenate(cand, axis=0)
        cid = jnp.concatenate(cid, axis=0)
        for r in range(K):
            m = jnp.max(cand, axis=0, keepdims=True)
            fsel = jnp.min(jnp.where(cand == m, flat, float(K * K)), axis=0, keepdims=True)
            hit = flat == fsel
            ts_ref[r:r + 1, :] = m
            ti_ref[r:r + 1, :] = jnp.sum(jnp.where(hit, cid, 0.0), axis=0, keepdims=True)
            cand = jnp.where(hit, -jnp.inf, cand)
        ts = ts_ref[...]
        e = jnp.exp(ts - jnp.max(ts, axis=0, keepdims=True))
        gt_ref[h * K:(h + 1) * K, :] = e / jnp.sum(e, axis=0, keepdims=True)
        idt_ref[h * K:(h + 1) * K, :] = (ti_ref[...] * ROWS_PER_EXPERT).astype(i32)
    gate_ref[...] = gt_ref[...].T
    ids_ref[...] = lax.bitcast_convert_type(lax.bitcast_convert_type(idt_ref[...], f32).T, i32)


def _route(x1, wpq, sk1, sk2, tb=256):
    T = x1.shape[0]
    HK = PEER_HEADS * PEER_TOPK
    full = lambda a: pl.BlockSpec(a.shape, lambda i: (0, 0))
    return pl.pallas_call(
        _route_body,
        grid=(T // tb,),
        in_specs=[pl.BlockSpec((tb, D_MODEL), lambda i: (i, 0)), full(wpq), full(sk1), full(sk2)],
        out_specs=[pl.BlockSpec((tb, HK), lambda i: (i, 0)), pl.BlockSpec((tb, HK), lambda i: (i, 0))],
        out_shape=[jax.ShapeDtypeStruct((T, HK), i32), jax.ShapeDtypeStruct((T, HK), f32)],
        scratch_shapes=[pltpu.VMEM((PEER_TOPK, tb), f32)] * 6 + [
                        pltpu.VMEM((HK, tb), i32), pltpu.VMEM((HK, tb), f32)],
        compiler_params=_params("arbitrary"),
        name="route",
    )(x1, wpq, sk1, sk2)


SLOTS = PEER_HEADS * PEER_TOPK
G_ROWS = SLOTS * ROWS_PER_EXPERT
G_ROWS16 = 2 * G_ROWS
TOKENS_PER_STEP = 4


def _pack_table(t):
    E, D = t.shape
    tb = lax.bitcast_convert_type(t.astype(jnp.bfloat16), jnp.uint16).astype(jnp.uint32)
    tb = tb.reshape(E, ROWS_PER_EXPERT, 2, LANES)
    return lax.bitcast_convert_type((tb[:, :, 1] << 16) | tb[:, :, 0], i32).reshape(E * ROWS_PER_EXPERT, LANES)


def _gather_rows(rows_ref, t, tab_ref, g_ref):
    R = ROWS_PER_EXPERT
    for k in range(SLOTS):
        r0 = rows_ref[t, k]
        g_ref[R * k:R * (k + 1), :] = tab_ref[pl.ds(pl.multiple_of(r0, R), R), :]


def _split2(a):
    hi = a.astype(bf16)
    return jnp.concatenate([hi, (a - hi.astype(f32)).astype(bf16)], axis=0)


def _token_pipeline(tb, rows_ref, tab_ref, g_scr, issue, finish):
    for u in range(TOKENS_PER_STEP):
        _gather_rows(rows_ref, u, tab_ref, g_scr.at[u])

    def step(i, c):
        for u in range(TOKENS_PER_STEP):
            t = i * TOKENS_PER_STEP + u
            z = issue(t, pltpu.bitcast(g_scr[u], jnp.bfloat16))
            _gather_rows(rows_ref, jnp.minimum(t + TOKENS_PER_STEP, tb - 1), tab_ref, g_scr.at[u])
            finish(t, z)
        return c
    lax.fori_loop(0, tb // TOKENS_PER_STEP, step, 0)


def _peer_u_body(rows_ref, x_ref, gate_ref, tab_ref, mask_ref, fold_ref, w_ref, g_scr, r_scr):
    tb = x_ref.shape[0]

    def issue(t, gb):
        return lax.dot_general(_split2(x_ref[t]), gb, _NT, preferred_element_type=f32)

    def finish(t, z):
        r_scr[pl.ds(t, 1), :] = jnp.sum((z[0:8] + z[8:16]) * mask_ref[...], axis=0, keepdims=True)

    _token_pipeline(tb, rows_ref, tab_ref, g_scr, issue, finish)
    h = jnp.dot(r_scr[...], fold_ref[...], precision=HIGHEST, preferred_element_type=f32)
    w_ref[...] = gate_ref[...] * (0.5 * h * (1.0 + lax.erf(h * (2.0 ** -0.5))))


def _chunk_mask():
    return (np.arange(G_ROWS16)[None, :] % 8 == np.arange(8)[:, None]).astype(np.float32)


def _peer_u(rows, x1r, gate, tab, tb=256):
    T = rows.shape[0]
    fold = (np.arange(G_ROWS16)[:, None] // 8 == np.arange(SLOTS)[None, :]).astype(np.float32)
    full = lambda a: pl.BlockSpec(a.shape, lambda i: (0, 0))
    return pl.pallas_call(
        _peer_u_body,
        grid=(T // tb,),
        in_specs=[pl.BlockSpec((tb, SLOTS), lambda i: (i, 0), memory_space=pltpu.SMEM),
                  pl.BlockSpec((tb, 8, LANES), lambda i: (i, 0, 0)),
                  pl.BlockSpec((tb, SLOTS), lambda i: (i, 0)),
                  full(tab), pl.BlockSpec((8, G_ROWS16), lambda i: (0, 0)),
                  pl.BlockSpec((G_ROWS16, SLOTS), lambda i: (0, 0))],
        out_specs=pl.BlockSpec((tb, SLOTS), lambda i: (i, 0)),
        out_shape=jax.ShapeDtypeStruct((T, SLOTS), f32),
        scratch_shapes=[pltpu.VMEM((TOKENS_PER_STEP, G_ROWS, LANES), i32), pltpu.VMEM((tb, G_ROWS16), f32)],
        compiler_params=_params("arbitrary", vmem=TABLE_VMEM_LIMIT),
        name="peer_u",
    )(rows, x1r, gate, tab, jnp.asarray(_chunk_mask()), jnp.asarray(fold))


def _peer_v_body(rows_ref, w_ref, x_ref, tab_ref, exp_ref, mask_ref, g_ref, b_ref, o_ref, g_scr, wexp_scr):
    tb = x_ref.shape[0]
    wexp_scr[...] = jnp.dot(w_ref[...], exp_ref[...], precision=HIGHEST, preferred_element_type=f32)

    def issue(t, gb):
        return jnp.dot(_split2(mask_ref[...] * wexp_scr[pl.ds(t, 1), :]), gb, preferred_element_type=f32)

    def finish(t, z):
        o_ref[t] = z[0:8] + z[8:16]

    _token_pipeline(tb, rows_ref, tab_ref, g_scr, issue, finish)
    y = ALPHA * x_ref[...] + o_ref[...]
    n_el = float(D_MODEL)
    mu = jnp.sum(y, axis=(1, 2), keepdims=True) / n_el
    yc = y - mu
    var = jnp.sum(yc * yc, axis=(1, 2), keepdims=True) / n_el
    o_ref[...] = yc * lax.rsqrt(var + LN_EPS) * g_ref[...] + b_ref[...]


def _peer_v(rows, w, x1r, tab, g, b, tb=256):
    T = rows.shape[0]
    expand = (np.arange(SLOTS)[:, None] == np.arange(G_ROWS16)[None, :] // 8).astype(np.float32)
    full = lambda a: pl.BlockSpec(a.shape, lambda i: (0, 0))
    return pl.pallas_call(
        _peer_v_body,
        grid=(T // tb,),
        in_specs=[pl.BlockSpec((tb, SLOTS), lambda i: (i, 0), memory_space=pltpu.SMEM),
                  pl.BlockSpec((tb, SLOTS), lambda i: (i, 0)),
                  pl.BlockSpec((tb, 8, LANES), lambda i: (i, 0, 0)),
                  full(tab), pl.BlockSpec((SLOTS, G_ROWS16), lambda i: (0, 0)),
                  pl.BlockSpec((8, G_ROWS16), lambda i: (0, 0)),
                  pl.BlockSpec((8, LANES), lambda i: (0, 0)),
                  pl.BlockSpec((8, LANES), lambda i: (0, 0))],
        out_specs=pl.BlockSpec((tb, 8, LANES), lambda i: (i, 0, 0)),
        out_shape=jax.ShapeDtypeStruct((T, 8, LANES), f32),
        scratch_shapes=[pltpu.VMEM((TOKENS_PER_STEP, G_ROWS, LANES), i32), pltpu.VMEM((tb, G_ROWS16), f32)],
        compiler_params=_params("arbitrary", vmem=TABLE_VMEM_LIMIT),
        name="peer_v",
    )(rows, w, x1r, tab, jnp.asarray(expand), jnp.asarray(_chunk_mask()), g, b)


def _regroup_w_in(w):
    offs = np.cumsum(np.array(IN_SPLITS))[:-1].tolist()
    qa, ckv, qi, ki, wi, qb, fb, ib, gb, gate_a, gate_b = jnp.split(w, offs, axis=-1)
    pad = jnp.zeros((w.shape[0], LANES - IDX_DIM - IDX_HEADS), w.dtype)
    out = jnp.concatenate([qa, qi, qb, fb, ib, gb, gate_a, gate_b, ckv, ki, wi, pad], axis=-1)
    assert out.shape[-1] == PROJ_COLS
    return out.astype(bf16)


def kernel(x, w_in, kv_norm_g, w_uk, w_uv, rel_bias, lb_params, b_norm_g, w_br_a, w_br_b, w_o, ln1_g, ln1_b,
           w_pq, sub_keys1, sub_keys2, u_table, v_table, ln2_g, ln2_b):
    B, S, D = x.shape
    T = B * S
    lower_bounds = jnp.cumsum(jax.nn.softmax(lb_params.astype(f32), axis=0), axis=0)
    x2 = x.reshape(T, D)
    for l in range(DEPTH):
        proj = _project(x2, _regroup_w_in(w_in[l]))
        oa = _dsa(proj, kv_norm_g[l][None, :], w_uk[l], w_uv[l], rel_bias, B, S)
        ob = _hgrn(proj, lower_bounds[l][None, :], b_norm_g[l][None, :], B, S)
        x1 = _merge(oa, ob, proj, x2, w_br_a[l].astype(bf16), w_br_b[l].astype(bf16), w_o[l].astype(bf16),
                    ln1_g[l][None, :], ln1_b[l][None, :])
        ids, gate = _route(x1, w_pq[l].astype(bf16), sub_keys1[l].astype(bf16), sub_keys2[l].astype(bf16))
        x1r = x1.reshape(T, 8, LANES)
        w = _peer_u(ids, x1r, gate, _pack_table(u_table[l]))
        x2 = _peer_v(ids, w, x1r, _pack_table(v_table[l]), ln2_g[l].reshape(8, LANES),
                     ln2_b[l].reshape(8, LANES)).reshape(T, D)
    return x2.reshape(B, S, D)
```

```python
import functools
import math

import numpy as np
import jax
import jax.numpy as jnp
from jax import lax
from jax.experimental import pallas as pl
from jax.experimental.pallas import tpu as pltpu

f32 = jnp.float32
bf16 = jnp.bfloat16
i32 = jnp.int32
HIGHEST = lax.Precision.HIGHEST

D_MODEL = 1024
DEPTH = 1
CHUNK = 64
Q_BLOCK = 128
A_HEADS = 8
A_HEAD_DIM = 64
A_Q = A_HEADS * A_HEAD_DIM
A_LATENT = 128
IDX_HEADS = 8
IDX_DIM = 64
TOPK_MAX = 256
REL_BUCKETS = 32
REL_MAX_DIST = 128
B_HEADS = 4
B_KEY_DIM = 128
B_VAL_DIM = 128
B_K = B_HEADS * B_KEY_DIM
B_V = B_HEADS * B_VAL_DIM
PEER_HEADS = 8
PEER_KEYS = 128
PEER_QDIM = 256
PEER_TOPK = 16
ALPHA = (2 * DEPTH) ** 0.25
LN_EPS = 1e-5
RMS_EPS = 1e-6

IN_SPLITS = (A_Q, A_LATENT, IDX_HEADS * IDX_DIM, IDX_DIM, IDX_HEADS, B_K, B_K, B_V, B_V, D_MODEL, D_MODEL)

COL_QA, COL_QI, COL_QB, COL_FB, COL_IB, COL_GB = 0, 512, 1024, 1536, 2048, 2560
COL_GATE_A, COL_GATE_B, COL_CKV, COL_KIWI = 3072, 4096, 5120, 5248
PROJ_COLS = 5376
LANES = 128
ROWS_PER_EXPERT = 4

INT_MIN = -(2 ** 31)
NEG_BIG = -1e30
VMEM_LIMIT = 48 * 1024 * 1024
TABLE_VMEM_LIMIT = 56 * 1024 * 1024

_NT = (((1,), (1,)), ((), ()))


def _params(*sem, vmem=VMEM_LIMIT):
    return pltpu.CompilerParams(dimension_semantics=sem, vmem_limit_bytes=vmem)


def _proj_body(x_ref, w_ref, o_ref):
    o_ref[...] = jnp.dot(x_ref[...].astype(bf16), w_ref[...], preferred_element_type=f32)


def _project(x2, w, tm=256):
    T = x2.shape[0]
    return pl.pallas_call(
        _proj_body,
        grid=(T // tm,),
        in_specs=[pl.BlockSpec((tm, D_MODEL), lambda i: (i, 0)),
                  pl.BlockSpec((D_MODEL, PROJ_COLS), lambda i: (0, 0))],
        out_specs=pl.BlockSpec((tm, PROJ_COLS), lambda i: (i, 0)),
        out_shape=jax.ShapeDtypeStruct((T, PROJ_COLS), f32),
        compiler_params=_params("arbitrary"),
        name="project",
    )(x2, w)


def _hgrn_body(q_ref, f_ref, i_ref, g_ref, lb_ref, ng_ref, o_ref, st_ref, *, n_chunks):
    @pl.when(pl.program_id(1) == 0)
    def _():
        st_ref[...] = jnp.zeros_like(st_ref)

    lb = lb_ref[...]
    ng = ng_ref[...]
    row = lax.broadcasted_iota(i32, (CHUNK, CHUNK), 0)
    col = lax.broadcasted_iota(i32, (CHUNK, CHUNK), 1)
    causal = row >= col
    tri = causal.astype(f32)
    for n in range(n_chunks):
        sl = pl.ds(n * CHUNK, CHUNK)
        z = f_ref[sl, :]
        log_f = jnp.log(lb + (1.0 - lb) * jax.nn.sigmoid(z))
        kk = (1.0 - lb) * jax.nn.sigmoid(-z)
        bc = jnp.dot(tri, log_f, precision=HIGHEST, preferred_element_type=f32)
        b_mid = bc[CHUNK // 2 - 1:CHUNK // 2, :]
        b_last = bc[CHUNK - 1:CHUNK, :]
        q = q_ref[sl, :]
        v = i_ref[sl, :]
        g = g_ref[sl, :]
        q_in = (q * jnp.exp(bc - b_mid)).astype(bf16)
        k_in = (kk * jnp.exp(b_mid - bc)).astype(bf16)
        k_dec = kk * jnp.exp(b_last - bc)
        q_dec = (q * jnp.exp(bc)).astype(bf16)
        decay = jnp.exp(b_last)
        outs = []
        for h in range(B_HEADS):
            hs = slice(h * B_KEY_DIM, (h + 1) * B_KEY_DIM)
            attn = lax.dot_general(q_in[:, hs], k_in[:, hs], _NT, preferred_element_type=f32)
            attn = jnp.where(causal, attn, 0.0)
            vh = v[:, hs]
            o = jnp.dot(attn.astype(bf16), vh.astype(bf16), preferred_element_type=f32)
            st = st_ref[h]
            o = o + lax.dot_general(q_dec[:, hs], st.astype(bf16), _NT, preferred_element_type=f32)
            d_st = jnp.dot(vh.T.astype(bf16), k_dec[:, hs].astype(bf16), preferred_element_type=f32)
            st_ref[h] = decay[:, hs] * st + d_st
            ms = jnp.mean(o * o, axis=-1, keepdims=True)
            gh = g[:, hs]
            outs.append(o * lax.rsqrt(ms + RMS_EPS) * ng * gh * jax.nn.sigmoid(gh))
        o_ref[sl, :] = jnp.concatenate(outs, axis=1)


def _hgrn(proj, lb, norm_g, B, S, tc=256):
    nc = S // tc
    blk = lambda c: pl.BlockSpec((tc, B_K), lambda b, s, c=c: (b * nc + s, c))
    return pl.pallas_call(
        functools.partial(_hgrn_body, n_chunks=tc // CHUNK),
        grid=(B, nc),
        in_specs=[blk(COL_QB // B_K), blk(COL_FB // B_K), blk(COL_IB // B_K), blk(COL_GB // B_K),
                  pl.BlockSpec((1, B_K), lambda b, s: (0, 0)),
                  pl.BlockSpec((1, B_VAL_DIM), lambda b, s: (0, 0))],
        out_specs=pl.BlockSpec((tc, B_V), lambda b, s: (b * nc + s, 0)),
        out_shape=jax.ShapeDtypeStruct((B * S, B_V), f32),
        scratch_shapes=[pltpu.VMEM((B_HEADS, B_VAL_DIM, B_KEY_DIM), f32)],
        compiler_params=_params("arbitrary", "arbitrary"),
        name="hgrn",
    )(proj, proj, proj, proj, lb, norm_g)


def _dsa_body(qa_ref, qi_ref, kwq_ref, ckv_ref, kw_ref, g_ref, wuk_ref, wuv_ref, bt_ref, bfar_ref, o_ref,
              ckvn_ref, skey_ref, s_ref, mt_ref, lt_ref, acc_ref, qlat_ref, qih_ref, wq_ref, thr_ref, ilim_ref,
              *, S, n_sel):
    j = pl.program_id(1)
    QB = Q_BLOCK
    n_kb = S // QB
    scale = A_HEAD_DIM ** -0.5
    rowi = lax.broadcasted_iota(i32, (QB, QB), 0)
    coli = lax.broadcasted_iota(i32, (QB, QB), 1)

    @pl.when(j == 0)
    def _():
        def norm_blk(kb, c):
            sl = pl.ds(pl.multiple_of(kb * QB, QB), QB)
            cb = ckv_ref[sl, :]
            ms = jnp.mean(cb * cb, axis=-1, keepdims=True)
            ckvn_ref[sl, :] = (cb * lax.rsqrt(ms + RMS_EPS) * g_ref[...]).astype(bf16)
            return c
        lax.fori_loop(0, n_kb, norm_blk, 0)

    qa = qa_ref[...]
    qi = qi_ref[...]
    wq = kwq_ref[:, IDX_DIM:IDX_DIM + IDX_HEADS] * ((IDX_HEADS * IDX_DIM) ** -0.5)
    for h in range(A_HEADS):
        qh = qa[:, h * A_HEAD_DIM:(h + 1) * A_HEAD_DIM].astype(bf16)
        qlat_ref[h * QB:(h + 1) * QB, :] = jnp.dot(qh, wuk_ref[h], preferred_element_type=f32).astype(bf16)
    for h in range(IDX_HEADS):
        qih_ref[h * QB:(h + 1) * QB, :] = qi[:, h * IDX_DIM:(h + 1) * IDX_DIM].astype(bf16)
        wq_ref[h] = jnp.broadcast_to(wq[:, h:h + 1], (QB, QB))

    def index_block(kb, is_current):
        sl = pl.ds(pl.multiple_of(kb * QB, QB), QB)
        kblk = kw_ref[sl, :][:, :IDX_DIM].astype(bf16)
        lg_all = lax.dot_general(qih_ref[...], kblk, _NT, preferred_element_type=f32)
        sc = jnp.zeros((QB, QB), f32)
        for h in range(IDX_HEADS):
            sc = sc + wq_ref[h] * jnp.maximum(lg_all[h * QB:(h + 1) * QB], 0.0)
        bits = lax.bitcast_convert_type(sc, i32)
        key = bits ^ ((bits >> 31) & jnp.int32(0x7FFFFFFF))
        if is_current:
            key = jnp.where((rowi < CHUNK) & (coli >= CHUNK), jnp.int32(INT_MIN), key)
        skey_ref[kb] = key

    def idx_loop(kb, c):
        index_block(kb, False)
        return c
    lax.fori_loop(0, j, idx_loop, 0)
    index_block(j, True)

    def count_where(pred_fn):
        def body(kb, acc):
            return acc + jnp.where(pred_fn(kb, skey_ref[kb]), 1.0, 0.0)
        acc = lax.fori_loop(0, j + 1, body, jnp.zeros((QB, QB), f32))
        return jnp.sum(acc, axis=1, keepdims=True)

    def bcast(col):
        return jnp.broadcast_to(col, (QB, QB))

    thr_ref[...] = jnp.full((QB, 1), INT_MIN, i32)
    ilim_ref[...] = jnp.full((QB, 1), S, i32)
    kf = float(n_sel)

    @pl.when((j + 1) * QB > n_sel)
    def _():
        zero = jnp.zeros((QB, 1), i32)
        c0 = count_where(lambda kb, key: key >= 0)
        n_all = jnp.zeros((QB, 1), f32) + ((j + 1) * QB).astype(f32)
        thr = jnp.where(c0 >= kf, zero, jnp.int32(INT_MIN))
        c_thr = jnp.where(c0 >= kf, c0, n_all)

        def bit_step(i, carry):
            thr, c_thr = carry
            cand = thr | jnp.left_shift(jnp.int32(1), 30 - i)
            cb = bcast(cand)
            c = count_where(lambda kb, key: key >= cb)
            ok = c >= kf
            return jnp.where(ok, cand, thr), jnp.where(ok, c, c_thr)
        thr, c_thr = lax.fori_loop(0, 31, bit_step, (thr, c_thr))
        thr_ref[...] = thr

        @pl.when(jnp.max(c_thr) > kf)
        def _():
            tb = bcast(thr)
            need = kf - count_where(lambda kb, key: key > tb)
            n_bits = int(math.ceil(math.log2(S + 1)))

            def lim_step(i, lim):
                cand = lim + jnp.left_shift(jnp.int32(1), n_bits - 1 - i)
                cb = bcast(cand)
                c = count_where(lambda kb, key: (key == tb) & (kb * QB + coli < cb))
                return jnp.where(c <= need, cand, lim)
            ilim_ref[...] = lax.fori_loop(0, n_bits, lim_step, zero)

    mt_ref[...] = jnp.full(mt_ref.shape, NEG_BIG, f32)
    thr_b = bcast(thr_ref[...])
    lim_b = bcast(ilim_ref[...])

    def score_block(kb, bias_of_head):
        sl = pl.ds(pl.multiple_of(kb * QB, QB), QB)
        key = skey_ref[kb]
        sel = ((key > thr_b) | ((key == thr_b) & (kb * QB + coli < lim_b))) & (key > jnp.int32(INT_MIN))
        s_all = lax.dot_general(qlat_ref[...], ckvn_ref[sl, :], _NT, preferred_element_type=f32)
        for h in range(A_HEADS):
            s = s_all[h * QB:(h + 1) * QB] * scale + bias_of_head(h)
            s = jnp.where(sel, s, NEG_BIG)
            s_ref[kb, h] = s
            mt_ref[h] = jnp.maximum(mt_ref[h], s)

    def far_loop(kb, c):
        score_block(kb, lambda h: bfar_ref[h:h + 1, :])
        return c
    lax.fori_loop(0, jnp.maximum(j - 1, 0), far_loop, 0)

    @pl.when(j >= 1)
    def _():
        score_block(j - 1, lambda h: bt_ref[h, :, 0:QB])
    score_block(j, lambda h: bt_ref[h, :, QB:2 * QB])

    for h in range(A_HEADS):
        mt_ref[h] = bcast(jnp.max(mt_ref[h], axis=1, keepdims=True))
    lt_ref[...] = jnp.zeros(lt_ref.shape, f32)
    acc_ref[...] = jnp.zeros(acc_ref.shape, f32)

    def pv_block(kb, c):
        sl = pl.ds(pl.multiple_of(kb * QB, QB), QB)
        ps = []
        for h in range(A_HEADS):
            p = jnp.exp(s_ref[kb, h] - mt_ref[h])
            lt_ref[h] = lt_ref[h] + p
            ps.append(p.astype(bf16))
        acc_ref[...] = acc_ref[...] + jnp.dot(jnp.concatenate(ps, axis=0), ckvn_ref[sl, :],
                                              preferred_element_type=f32)
        return c
    lax.fori_loop(0, j + 1, pv_block, 0)

    outs = []
    for h in range(A_HEADS):
        o_lat = acc_ref[h * QB:(h + 1) * QB, :] / jnp.sum(lt_ref[h], axis=1, keepdims=True)
        outs.append(jnp.dot(o_lat.astype(bf16), wuv_ref[h], preferred_element_type=f32))
    o_ref[...] = jnp.concatenate(outs, axis=1)


def _t5_bucket(rel):
    half = REL_BUCKETS // 2
    max_exact = half // 2
    base = jnp.where(rel > 0, half, 0)
    n = jnp.abs(rel)
    nf = jnp.maximum(n, 1).astype(f32)
    large = max_exact + (jnp.log(nf / max_exact) / math.log(REL_MAX_DIST / max_exact) * (half - max_exact)).astype(i32)
    large = jnp.minimum(large, half - 1)
    return base + jnp.where(n < max_exact, n, large)


def _dsa(proj, kv_norm_g, w_uk, w_uv, rel_bias, B, S):
    QB = Q_BLOCK
    nq = S // QB
    n_sel = min(TOPK_MAX, S // 4)
    assert REL_MAX_DIST <= QB
    r = jnp.arange(QB, dtype=i32)[:, None]
    c = jnp.arange(2 * QB, dtype=i32)[None, :]
    bias_tile = jnp.moveaxis(rel_bias[_t5_bucket(c - QB - r)], -1, 0).astype(f32)
    bias_far = jnp.broadcast_to(rel_bias[_t5_bucket(jnp.int32(-QB - 1))][:, None], (A_HEADS, LANES)).astype(f32)
    wuk_t = jnp.transpose(w_uk, (1, 2, 0)).astype(bf16)
    wuv_h = jnp.transpose(w_uv, (1, 0, 2)).astype(bf16)
    qblk = lambda width, cb: pl.BlockSpec((QB, width), lambda b, j, cb=cb: (b * nq + j, cb))
    seq = lambda cb: pl.BlockSpec((S, LANES), lambda b, j, cb=cb: (b, cb))
    full = lambda a: pl.BlockSpec(a.shape, lambda b, j, n=a.ndim: (0,) * n)
    return pl.pallas_call(
        functools.partial(_dsa_body, S=S, n_sel=n_sel),
        grid=(B, nq),
        in_specs=[qblk(A_Q, COL_QA // A_Q), qblk(A_Q, COL_QI // A_Q), qblk(LANES, COL_KIWI // LANES),
                  seq(COL_CKV // LANES), seq(COL_KIWI // LANES),
                  full(kv_norm_g), full(wuk_t), full(wuv_h), full(bias_tile), full(bias_far)],
        out_specs=pl.BlockSpec((QB, A_Q), lambda b, j: (b * nq + j, 0)),
        out_shape=jax.ShapeDtypeStruct((B * S, A_Q), f32),
        scratch_shapes=[
            pltpu.VMEM((S, A_LATENT), bf16),
            pltpu.VMEM((S // QB, QB, QB), i32),
            pltpu.VMEM((S // QB, A_HEADS, QB, QB), f32),
            pltpu.VMEM((A_HEADS, QB, QB), f32),
            pltpu.VMEM((A_HEADS, QB, QB), f32),
            pltpu.VMEM((A_HEADS * QB, A_LATENT), f32),
            pltpu.VMEM((A_HEADS * QB, A_LATENT), bf16),
            pltpu.VMEM((IDX_HEADS * QB, IDX_DIM), bf16),
            pltpu.VMEM((IDX_HEADS, QB, QB), f32),
            pltpu.VMEM((QB, 1), i32),
            pltpu.VMEM((QB, 1), i32),
        ],
        compiler_params=_params("arbitrary", "arbitrary"),
        name="dsa",
    )(proj, proj, proj, proj, proj, kv_norm_g, wuk_t, wuv_h, bias_tile, bias_far)


def _paired(n, fn, init=None):
    def pair(i, c):
        return fn(2 * i + 1, fn(2 * i, c))
    c = lax.fori_loop(0, lax.shift_right_logical(n, 1), pair, init)
    return lax.cond((n & 1) == 1, lambda c: fn(n - 1, c), lambda c: c, c)


def _dsa_t_body(qa_ref, qi_ref, kwq_ref, ckv_ref, kw_ref, g_ref, wuk_ref, wuvt_ref, btt_ref, bfar_ref, o_ref,
                ckvn_ref, ckvnt_ref, skey_ref, s_ref, mt_ref, lt_ref, acc_ref, qlat_ref, qih_ref, wqt_ref,
                mrow_ref, thr_ref, ilim_ref, *, S, n_sel):
    j = pl.program_id(1)
    QB = Q_BLOCK
    n_kb = S // QB
    scale = A_HEAD_DIM ** -0.5
    keyi = lax.broadcasted_iota(i32, (QB, QB), 0)
    qryi = lax.broadcasted_iota(i32, (QB, QB), 1)

    def rows(kb):
        return pl.ds(pl.multiple_of(kb * QB, QB), QB)

    @pl.when(j == 0)
    def _():
        def norm_blk(kb, c):
            cb = ckv_ref[rows(kb), :]
            ms = jnp.mean(cb * cb, axis=-1, keepdims=True)
            cn = cb * lax.rsqrt(ms + RMS_EPS) * g_ref[...]
            ckvn_ref[rows(kb), :] = cn.astype(bf16)
            ckvnt_ref[kb] = cn.T.astype(bf16)
            return c
        lax.fori_loop(0, n_kb, norm_blk, 0)

    qa = qa_ref[...]
    qi = qi_ref[...]
    wqt_ref[...] = kwq_ref[...].T[IDX_DIM:IDX_DIM + IDX_HEADS, :] * ((IDX_HEADS * IDX_DIM) ** -0.5)
    for h in range(A_HEADS):
        qh = qa[:, h * A_HEAD_DIM:(h + 1) * A_HEAD_DIM].astype(bf16)
        qlat_ref[h * QB:(h + 1) * QB, :] = jnp.dot(qh, wuk_ref[h], preferred_element_type=f32).astype(bf16)
    for h in range(IDX_HEADS):
        qih_ref[h * QB:(h + 1) * QB, :] = qi[:, h * IDX_DIM:(h + 1) * IDX_DIM].astype(bf16)

    def index_block(kb, is_current):
        kblk = kw_ref[rows(kb), :][:, :IDX_DIM].astype(bf16)
        lg_all = lax.dot_general(kblk, qih_ref[...], _NT, preferred_element_type=f32)
        sc = jnp.zeros((QB, QB), f32)
        for h in range(IDX_HEADS):
            sc = sc + wqt_ref[h:h + 1, :] * jnp.maximum(lg_all[:, h * QB:(h + 1) * QB], 0.0)
        bits = lax.bitcast_convert_type(sc, i32)
        key = bits ^ ((bits >> 31) & jnp.int32(0x7FFFFFFF))
        if is_current:
            key = jnp.where((qryi < CHUNK) & (keyi >= CHUNK), jnp.int32(INT_MIN), key)
        skey_ref[kb] = key

    def idx_loop(kb, c):
        index_block(kb, False)
        return c
    _paired(j, idx_loop)
    index_block(j, True)

    def count_where(pred_fn):
        def body(kb, acc):
            return acc + jnp.where(pred_fn(kb, skey_ref[kb]), 1.0, 0.0)
        acc = _paired(j + 1, body, jnp.zeros((QB, QB), f32))
        return jnp.sum(acc, axis=0, keepdims=True)

    thr_ref[...] = jnp.full((1, QB), INT_MIN, i32)
    ilim_ref[...] = jnp.full((1, QB), S, i32)
    kf = float(n_sel)

    @pl.when((j + 1) * QB > n_sel)
    def _():
        zero = jnp.zeros((1, QB), i32)
        c0 = count_where(lambda kb, key: key >= 0)
        n_all = jnp.zeros((1, QB), f32) + ((j + 1) * QB).astype(f32)
        thr = jnp.where(c0 >= kf, zero, jnp.int32(INT_MIN))
        c_thr = jnp.where(c0 >= kf, c0, n_all)

        def bit_step(i, carry):
            thr, c_thr = carry
            cand = thr | jnp.left_shift(jnp.int32(1), 30 - i)
            c = count_where(lambda kb, key: key >= cand)
            ok = c >= kf
            return jnp.where(ok, cand, thr), jnp.where(ok, c, c_thr)
        thr, c_thr = lax.fori_loop(0, 31, bit_step, (thr, c_thr))
        thr_ref[...] = thr

        @pl.when(jnp.max(c_thr) > kf)
        def _():
            need = kf - count_where(lambda kb, key: key > thr)
            n_bits = int(math.ceil(math.log2(S + 1)))

            def lim_step(i, lim):
                cand = lim + jnp.left_shift(jnp.int32(1), n_bits - 1 - i)
                c = count_where(lambda kb, key: (key == thr) & (kb * QB + keyi < cand))
                return jnp.where(c <= need, cand, lim)
            ilim_ref[...] = lax.fori_loop(0, n_bits, lim_step, zero)

    mt_ref[...] = jnp.full(mt_ref.shape, NEG_BIG, f32)
    thr = thr_ref[...]
    lim = ilim_ref[...]

    def score_block(kb, bias_of_head):
        key = skey_ref[kb]
        sel = ((key > thr) | ((key == thr) & (kb * QB + keyi < lim))) & (key > jnp.int32(INT_MIN))
        s_all = lax.dot_general(ckvn_ref[rows(kb), :], qlat_ref[...], _NT, preferred_element_type=f32)
        for h in range(A_HEADS):
            s = s_all[:, h * QB:(h + 1) * QB] * scale + bias_of_head(h)
            s = jnp.where(sel, s, NEG_BIG)
            s_ref[kb, h] = s
            mt_ref[h] = jnp.maximum(mt_ref[h], s)

    def far_loop(kb, c):
        score_block(kb, lambda h: bfar_ref[h:h + 1, :])
        return c
    _paired(jnp.maximum(j - 1, 0), far_loop)

    @pl.when(j >= 1)
    def _():
        score_block(j - 1, lambda h: btt_ref[h, 0:QB, :])
    score_block(j, lambda h: btt_ref[h, QB:2 * QB, :])

    for h in range(A_HEADS):
        mrow_ref[h:h + 1, :] = jnp.max(mt_ref[h], axis=0, keepdims=True)
    lt_ref[...] = jnp.zeros(lt_ref.shape, f32)
    acc_ref[...] = jnp.zeros(acc_ref.shape, f32)

    def pv_block(kb, c):
        ps = []
        for h in range(A_HEADS):
            p = jnp.exp(s_ref[kb, h] - mrow_ref[h:h + 1, :])
            lt_ref[h] = lt_ref[h] + p
            ps.append(p.astype(bf16))
        acc_ref[...] = acc_ref[...] + jnp.dot(ckvnt_ref[kb], jnp.concatenate(ps, axis=1),
                                              preferred_element_type=f32)
        return c
    _paired(j + 1, pv_block)

    outs = []
    for h in range(A_HEADS):
        o_lat_t = acc_ref[:, h * QB:(h + 1) * QB] / jnp.sum(lt_ref[h], axis=0, keepdims=True)
        outs.append(jnp.dot(wuvt_ref[h], o_lat_t.astype(bf16), preferred_element_type=f32))
    o_ref[...] = jnp.concatenate(outs, axis=0).T


def _dsa_t(proj, kv_norm_g, w_uk, w_uv, rel_bias, B, S):
    QB = Q_BLOCK
    nq = S // QB
    n_sel = min(TOPK_MAX, S // 4)
    assert REL_MAX_DIST <= QB
    c = jnp.arange(2 * QB, dtype=i32)[:, None]
    r = jnp.arange(QB, dtype=i32)[None, :]
    bias_tile_t = jnp.moveaxis(rel_bias[_t5_bucket(c - QB - r)], -1, 0).astype(f32)
    bias_far = jnp.broadcast_to(rel_bias[_t5_bucket(jnp.int32(-QB - 1))][:, None], (A_HEADS, LANES)).astype(f32)
    wuk_t = jnp.transpose(w_uk, (1, 2, 0)).astype(bf16)
    wuv_t = jnp.transpose(w_uv, (1, 2, 0)).astype(bf16)
    qblk = lambda width, cb: pl.BlockSpec((QB, width), lambda b, j, cb=cb: (b * nq + j, cb))
    seq = lambda cb: pl.BlockSpec((S, LANES), lambda b, j, cb=cb: (b, cb))
    full = lambda a: pl.BlockSpec(a.shape, lambda b, j, n=a.ndim: (0,) * n)
    return pl.pallas_call(
        functools.partial(_dsa_t_body, S=S, n_sel=n_sel),
        grid=(B, nq),
        in_specs=[qblk(A_Q, COL_QA // A_Q), qblk(A_Q, COL_QI // A_Q), qblk(LANES, COL_KIWI // LANES),
                  seq(COL_CKV // LANES), seq(COL_KIWI // LANES),
                  full(kv_norm_g), full(wuk_t), full(wuv_t), full(bias_tile_t), full(bias_far)],
        out_specs=pl.BlockSpec((QB, A_Q), lambda b, j: (b * nq + j, 0)),
        out_shape=jax.ShapeDtypeStruct((B * S, A_Q), f32),
        scratch_shapes=[
            pltpu.VMEM((S, A_LATENT), bf16),
            pltpu.VMEM((S // QB, A_LATENT, QB), bf16),
            pltpu.VMEM((S // QB, QB, QB), i32),
            pltpu.VMEM((S // QB, A_HEADS, QB, QB), f32),
            pltpu.VMEM((A_HEADS, QB, QB), f32),
            pltpu.VMEM((A_HEADS, QB, QB), f32),
            pltpu.VMEM((A_LATENT, A_HEADS * QB), f32),
            pltpu.VMEM((A_HEADS * QB, A_LATENT), bf16),
            pltpu.VMEM((IDX_HEADS * QB, IDX_DIM), bf16),
            pltpu.VMEM((IDX_HEADS, QB), f32),
            pltpu.VMEM((A_HEADS, QB), f32),
            pltpu.VMEM((1, QB), i32),
            pltpu.VMEM((1, QB), i32),
        ],
        compiler_params=_params("arbitrary", "arbitrary"),
        name="dsa",
    )(proj, proj, proj, proj, proj, kv_norm_g, wuk_t, wuv_t, bias_tile_t, bias_far)


def _layer_norm(y, g, b):
    mu = jnp.mean(y, axis=-1, keepdims=True)
    var = jnp.mean(jnp.square(y - mu), axis=-1, keepdims=True)
    return (y - mu) * lax.rsqrt(var + LN_EPS) * g + b


def _merge_body(oa_ref, ob_ref, ga_ref, gb_ref, x_ref, wa_ref, wb_ref, wo_ref, g_ref, b_ref, o_ref, ot_ref):
    ya = jnp.dot(oa_ref[...].astype(bf16), wa_ref[...], preferred_element_type=f32)
    yb = jnp.dot(ob_ref[...].astype(bf16), wb_ref[...], preferred_element_type=f32)
    merged = jax.nn.sigmoid(ga_ref[...]) * ya + jax.nn.sigmoid(gb_ref[...]) * yb
    mix = jnp.dot(merged.astype(bf16), wo_ref[...], preferred_element_type=f32)
    y = _layer_norm(ALPHA * x_ref[...] + mix, g_ref[...], b_ref[...])
    o_ref[...] = y
    for q in range(D_MODEL // LANES):
        ot_ref[:, q, :] = y[:, q * LANES:(q + 1) * LANES]


def _merge(oa, ob, proj, x2, wa, wb, wo, g, b, tm=256):
    T = x2.shape[0]
    full = lambda a: pl.BlockSpec(a.shape, lambda i: (0, 0))
    return pl.pallas_call(
        _merge_body,
        grid=(T // tm,),
        in_specs=[pl.BlockSpec((tm, A_Q), lambda i: (i, 0)), pl.BlockSpec((tm, B_V), lambda i: (i, 0)),
                  pl.BlockSpec((tm, D_MODEL), lambda i: (i, COL_GATE_A // D_MODEL)),
                  pl.BlockSpec((tm, D_MODEL), lambda i: (i, COL_GATE_B // D_MODEL)),
                  pl.BlockSpec((tm, D_MODEL), lambda i: (i, 0)),
                  full(wa), full(wb), full(wo), full(g), full(b)],
        out_specs=[pl.BlockSpec((tm, D_MODEL), lambda i: (i, 0)),
                   pl.BlockSpec((tm, D_MODEL // LANES, LANES), lambda i: (i, 0, 0))],
        out_shape=[jax.ShapeDtypeStruct((T, D_MODEL), f32),
                   jax.ShapeDtypeStruct((T, D_MODEL // LANES, LANES), f32)],
        compiler_params=_params("arbitrary"),
        name="merge",
    )(oa, ob, proj, proj, x2, wa, wb, wo, g, b)


def _route_body(x_ref, wpq_ref, sk1_ref, sk2_ref, ids_ref, gate_ref,
                v1_ref, i1_ref, v2_ref, i2_ref, ts_ref, ti_ref, idt_ref, gt_ref):
    tb = x_ref.shape[0]
    K = PEER_TOPK
    q = jnp.dot(x_ref[...].astype(bf16), wpq_ref[...], preferred_element_type=f32)
    kiota = lax.broadcasted_iota(i32, (PEER_KEYS, tb), 0).astype(f32)
    r8 = lax.broadcasted_iota(i32, (8, tb), 0).astype(f32)
    r16 = lax.broadcasted_iota(i32, (16, tb), 0).astype(f32)
    flat = jnp.concatenate([r16] + [K * a + r8 for a in range(1, 8)] + [K * (8 + r8)], axis=0)
    half = PEER_QDIM // 2
    for h in range(PEER_HEADS):
        for part, (sk_ref, v_ref, i_ref) in enumerate(((sk1_ref, v1_ref, i1_ref), (sk2_ref, v2_ref, i2_ref))):
            off = h * PEER_QDIM + part * half
            qh = q[:, off:off + half].astype(bf16)
            s = lax.dot_general(sk_ref[...], qh, _NT, preferred_element_type=f32)
            for r in range(K):
                m = jnp.max(s, axis=0, keepdims=True)
                ix = jnp.min(jnp.where(s == m, kiota, float(PEER_KEYS)), axis=0, keepdims=True)
                v_ref[r:r + 1, :] = m
                i_ref[r:r + 1, :] = ix
                s = jnp.where(kiota == ix, -jnp.inf, s)
        v1, i1, v2, i2 = v1_ref[...], i1_ref[...], v2_ref[...], i2_ref[...]
        cand = [v1[0:1] + v2, ] + [v1[a:a + 1] + v2[0:8] for a in range(1, 8)] + [v1[8:16] + v2[0:1]]
        cid = [i1[0:1] * PEER_KEYS + i2, ] + [i1[a:a + 1] * PEER_KEYS + i2[0:8] for a in range(1, 8)] \
            + [i1[8:16] * PEER_KEYS + i2[0:1]]
        cand = jnp.concatenate(cand, axis=0)
        cid = jnp.concatenate(cid, axis=0)
        for r in range(K):
            m = jnp.max(cand, axis=0, keepdims=True)
            fsel = jnp.min(jnp.where(cand == m, flat, float(K * K)), axis=0, keepdims=True)
            hit = flat == fsel
            ts_ref[r:r + 1, :] = m
            ti_ref[r:r + 1, :] = jnp.sum(jnp.where(hit, cid, 0.0), axis=0, keepdims=True)
            cand = jnp.where(hit, -jnp.inf, cand)
        ts = ts_ref[...]
        e = jnp.exp(ts - jnp.max(ts, axis=0, keepdims=True))
        gt_ref[h * K:(h + 1) * K, :] = e / jnp.sum(e, axis=0, keepdims=True)
        idt_ref[h * K:(h + 1) * K, :] = (ti_ref[...] * ROWS_PER_EXPERT).astype(i32)
    gate_ref[...] = gt_ref[...].T
    ids_ref[...] = lax.bitcast_convert_type(lax.bitcast_convert_type(idt_ref[...], f32).T, i32)


def _route(x1, wpq, sk1, sk2, tb=256):
    T = x1.shape[0]
    HK = PEER_HEADS * PEER_TOPK
    full = lambda a: pl.BlockSpec(a.shape, lambda i: (0, 0))
    return pl.pallas_call(
        _route_body,
        grid=(T // tb,),
        in_specs=[pl.BlockSpec((tb, D_MODEL), lambda i: (i, 0)), full(wpq), full(sk1), full(sk2)],
        out_specs=[pl.BlockSpec((tb, HK), lambda i: (i, 0)), pl.BlockSpec((tb, HK), lambda i: (i, 0))],
        out_shape=[jax.ShapeDtypeStruct((T, HK), i32), jax.ShapeDtypeStruct((T, HK), f32)],
        scratch_shapes=[pltpu.VMEM((PEER_TOPK, tb), f32)] * 6 + [
                        pltpu.VMEM((HK, tb), i32), pltpu.VMEM((HK, tb), f32)],
        compiler_params=_params("arbitrary"),
        name="route",
    )(x1, wpq, sk1, sk2)


SLOTS = PEER_HEADS * PEER_TOPK
G_ROWS = SLOTS * ROWS_PER_EXPERT
G_ROWS16 = 2 * G_ROWS
TOKENS_PER_STEP = 4


def _pack_table(t):
    E, D = t.shape
    tb = lax.bitcast_convert_type(t.astype(jnp.bfloat16), jnp.uint16).astype(jnp.uint32)
    tb = tb.reshape(E, ROWS_PER_EXPERT, 2, LANES)
    return lax.bitcast_convert_type((tb[:, :, 1] << 16) | tb[:, :, 0], i32).reshape(E * ROWS_PER_EXPERT, LANES)


def _gather_rows(rows_ref, t, tab_ref, g_ref):
    R = ROWS_PER_EXPERT
    for k in range(SLOTS):
        r0 = rows_ref[t, k]
        g_ref[R * k:R * (k + 1), :] = tab_ref[pl.ds(pl.multiple_of(r0, R), R), :]


def _split2(a):
    hi = a.astype(bf16)
    return jnp.concatenate([hi, (a - hi.astype(f32)).astype(bf16)], axis=0)


def _token_pipeline(tb, rows_ref, tab_ref, g_scr, issue, finish):
    for u in range(TOKENS_PER_STEP):
        _gather_rows(rows_ref, u, tab_ref, g_scr.at[u])

    def step(i, c):
        for u in range(TOKENS_PER_STEP):
            t = i * TOKENS_PER_STEP + u
            z = issue(t, pltpu.bitcast(g_scr[u], jnp.bfloat16))
            _gather_rows(rows_ref, jnp.minimum(t + TOKENS_PER_STEP, tb - 1), tab_ref, g_scr.at[u])
            finish(t, z)
        return c
    lax.fori_loop(0, tb // TOKENS_PER_STEP, step, 0)


def _peer_u_body(rows_ref, x_ref, gate_ref, tab_ref, mask_ref, fold_ref, w_ref, g_scr, r_scr):
    tb = x_ref.shape[0]

    def issue(t, gb):
        return lax.dot_general(_split2(x_ref[t]), gb, _NT, preferred_element_type=f32)

    def finish(t, z):
        r_scr[pl.ds(t, 1), :] = jnp.sum((z[0:8] + z[8:16]) * mask_ref[...], axis=0, keepdims=True)

    _token_pipeline(tb, rows_ref, tab_ref, g_scr, issue, finish)
    h = jnp.dot(r_scr[...], fold_ref[...], precision=HIGHEST, preferred_element_type=f32)
    w_ref[...] = gate_ref[...] * (0.5 * h * (1.0 + lax.erf(h * (2.0 ** -0.5))))


def _chunk_mask():
    return (np.arange(G_ROWS16)[None, :] % 8 == np.arange(8)[:, None]).astype(np.float32)


def _peer_u(rows, x1r, gate, tab, tb=256):
    T = rows.shape[0]
    fold = (np.arange(G_ROWS16)[:, None] // 8 == np.arange(SLOTS)[None, :]).astype(np.float32)
    full = lambda a: pl.BlockSpec(a.shape, lambda i: (0, 0))
    return pl.pallas_call(
        _peer_u_body,
        grid=(T // tb,),
        in_specs=[pl.BlockSpec((tb, SLOTS), lambda i: (i, 0), memory_space=pltpu.SMEM),
                  pl.BlockSpec((tb, 8, LANES), lambda i: (i, 0, 0)),
                  pl.BlockSpec((tb, SLOTS), lambda i: (i, 0)),
                  full(tab), pl.BlockSpec((8, G_ROWS16), lambda i: (0, 0)),
                  pl.BlockSpec((G_ROWS16, SLOTS), lambda i: (0, 0))],
        out_specs=pl.BlockSpec((tb, SLOTS), lambda i: (i, 0)),
        out_shape=jax.ShapeDtypeStruct((T, SLOTS), f32),
        scratch_shapes=[pltpu.VMEM((TOKENS_PER_STEP, G_ROWS, LANES), i32), pltpu.VMEM((tb, G_ROWS16), f32)],
        compiler_params=_params("arbitrary", vmem=TABLE_VMEM_LIMIT),
        name="peer_u",
    )(rows, x1r, gate, tab, jnp.asarray(_chunk_mask()), jnp.asarray(fold))


def _peer_v_body(rows_ref, w_ref, x_ref, tab_ref, exp_ref, mask_ref, g_ref, b_ref, o_ref, g_scr, wexp_scr, y_scr):
    tb = x_ref.shape[0]
    wexp_scr[...] = jnp.dot(w_ref[...], exp_ref[...], precision=HIGHEST, preferred_element_type=f32)

    def issue(t, gb):
        return jnp.dot(_split2(mask_ref[...] * wexp_scr[pl.ds(t, 1), :]), gb, preferred_element_type=f32)

    def finish(t, z):
        y_scr[t] = z[0:8] + z[8:16]

    _token_pipeline(tb, rows_ref, tab_ref, g_scr, issue, finish)
    y = ALPHA * x_ref[...] + y_scr[...]
    n_el = float(D_MODEL)
    mu = jnp.sum(y, axis=(1, 2), keepdims=True) / n_el
    yc = y - mu
    var = jnp.sum(yc * yc, axis=(1, 2), keepdims=True) / n_el
    y_scr[...] = yc * lax.rsqrt(var + LN_EPS) * g_ref[...] + b_ref[...]
    for q in range(D_MODEL // LANES):
        o_ref[:, q * LANES:(q + 1) * LANES] = y_scr[:, q, :]


def _peer_v(rows, w, x1r, tab, g, b, tb=256):
    T = rows.shape[0]
    expand = (np.arange(SLOTS)[:, None] == np.arange(G_ROWS16)[None, :] // 8).astype(np.float32)
    full = lambda a: pl.BlockSpec(a.shape, lambda i: (0, 0))
    return pl.pallas_call(
        _peer_v_body,
        grid=(T // tb,),
        in_specs=[pl.BlockSpec((tb, SLOTS), lambda i: (i, 0), memory_space=pltpu.SMEM),
                  pl.BlockSpec((tb, SLOTS), lambda i: (i, 0)),
                  pl.BlockSpec((tb, 8, LANES), lambda i: (i, 0, 0)),
                  full(tab), pl.BlockSpec((SLOTS, G_ROWS16), lambda i: (0, 0)),
                  pl.BlockSpec((8, G_ROWS16), lambda i: (0, 0)),
                  pl.BlockSpec((8, LANES), lambda i: (0, 0)),
                  pl.BlockSpec((8, LANES), lambda i: (0, 0))],
        out_specs=pl.BlockSpec((tb, D_MODEL), lambda i: (i, 0)),
        out_shape=jax.ShapeDtypeStruct((T, D_MODEL), f32),
        scratch_shapes=[pltpu.VMEM((TOKENS_PER_STEP, G_ROWS, LANES), i32), pltpu.VMEM((tb, G_ROWS16), f32),
                        pltpu.VMEM((tb, 8, LANES), f32)],
        compiler_params=_params("arbitrary", vmem=TABLE_VMEM_LIMIT),
        name="peer_v",
    )(rows, w, x1r, tab, jnp.asarray(expand), jnp.asarray(_chunk_mask()), g, b)


def _regroup_w_in(w):
    offs = np.cumsum(np.array(IN_SPLITS))[:-1].tolist()
    qa, ckv, qi, ki, wi, qb, fb, ib, gb, gate_a, gate_b = jnp.split(w, offs, axis=-1)
    pad = jnp.zeros((w.shape[0], LANES - IDX_DIM - IDX_HEADS), w.dtype)
    out = jnp.concatenate([qa, qi, qb, fb, ib, gb, gate_a, gate_b, ckv, ki, wi, pad], axis=-1)
    assert out.shape[-1] == PROJ_COLS
    return out.astype(bf16)


def kernel(x, w_in, kv_norm_g, w_uk, w_uv, rel_bias, lb_params, b_norm_g, w_br_a, w_br_b, w_o, ln1_g, ln1_b,
           w_pq, sub_keys1, sub_keys2, u_table, v_table, ln2_g, ln2_b):
    B, S, D = x.shape
    T = B * S
    lower_bounds = jnp.cumsum(jax.nn.softmax(lb_params.astype(f32), axis=0), axis=0)
    x2 = x.reshape(T, D)
    for l in range(DEPTH):
        proj = _project(x2, _regroup_w_in(w_in[l]))
        oa = _dsa_t(proj, kv_norm_g[l][None, :], w_uk[l], w_uv[l], rel_bias, B, S)
        ob = _hgrn(proj, lower_bounds[l][None, :], b_norm_g[l][None, :], B, S)
        x1, x1r = _merge(oa, ob, proj, x2, w_br_a[l].astype(bf16), w_br_b[l].astype(bf16), w_o[l].astype(bf16),
                         ln1_g[l][None, :], ln1_b[l][None, :])
        ids, gate = _route(x1, w_pq[l].astype(bf16), sub_keys1[l].astype(bf16), sub_keys2[l].astype(bf16))
        w = _peer_u(ids, x1r, gate, _pack_table(u_table[l]))
        x2 = _peer_v(ids, w, x1r, _pack_table(v_table[l]), ln2_g[l].reshape(8, LANES),
                     ln2_b[l].reshape(8, LANES)).reshape(T, D)
    return x2.reshape(B, S, D)
```

```python
import functools
import math

import numpy as np
import jax
import jax.numpy as jnp
from jax import lax
from jax.experimental import pallas as pl
from jax.experimental.pallas import tpu as pltpu

f32 = jnp.float32
bf16 = jnp.bfloat16
i32 = jnp.int32
HIGHEST = lax.Precision.HIGHEST

D_MODEL = 1024
DEPTH = 1
CHUNK = 64
Q_BLOCK = 128
A_HEADS = 8
A_HEAD_DIM = 64
A_Q = A_HEADS * A_HEAD_DIM
A_LATENT = 128
IDX_HEADS = 8
IDX_DIM = 64
TOPK_MAX = 256
REL_BUCKETS = 32
REL_MAX_DIST = 128
B_HEADS = 4
B_KEY_DIM = 128
B_VAL_DIM = 128
B_K = B_HEADS * B_KEY_DIM
B_V = B_HEADS * B_VAL_DIM
PEER_HEADS = 8
PEER_KEYS = 128
PEER_QDIM = 256
PEER_TOPK = 16
ALPHA = (2 * DEPTH) ** 0.25
LN_EPS = 1e-5
RMS_EPS = 1e-6

IN_SPLITS = (A_Q, A_LATENT, IDX_HEADS * IDX_DIM, IDX_DIM, IDX_HEADS, B_K, B_K, B_V, B_V, D_MODEL, D_MODEL)

COL_QA, COL_QI, COL_QB, COL_FB, COL_IB, COL_GB = 0, 512, 1024, 1536, 2048, 2560
COL_GATE_A, COL_GATE_B, COL_CKV, COL_KIWI = 3072, 4096, 5120, 5248
PROJ_COLS = 5376
LANES = 128
ROWS_PER_EXPERT = 4

INT_MIN = -(2 ** 31)
NEG_BIG = -1e30
VMEM_LIMIT = 48 * 1024 * 1024
TABLE_VMEM_LIMIT = 56 * 1024 * 1024

_NT = (((1,), (1,)), ((), ()))


def _params(*sem, vmem=VMEM_LIMIT):
    return pltpu.CompilerParams(dimension_semantics=sem, vmem_limit_bytes=vmem)


def _proj_body(x_ref, w_ref, o_ref):
    o_ref[...] = jnp.dot(x_ref[...].astype(bf16), w_ref[...], preferred_element_type=f32)


def _project(x2, w, tm=256):
    T = x2.shape[0]
    return pl.pallas_call(
        _proj_body,
        grid=(T // tm,),
        in_specs=[pl.BlockSpec((tm, D_MODEL), lambda i: (i, 0)),
                  pl.BlockSpec((D_MODEL, PROJ_COLS), lambda i: (0, 0))],
        out_specs=pl.BlockSpec((tm, PROJ_COLS), lambda i: (i, 0)),
        out_shape=jax.ShapeDtypeStruct((T, PROJ_COLS), f32),
        compiler_params=_params("arbitrary"),
        name="project",
    )(x2, w)


def _hgrn_body(q_ref, f_ref, i_ref, g_ref, lb_ref, ng_ref, o_ref, st_ref, *, n_chunks):
    @pl.when(pl.program_id(1) == 0)
    def _():
        st_ref[...] = jnp.zeros_like(st_ref)

    lb = lb_ref[...]
    ng = ng_ref[...]
    row = lax.broadcasted_iota(i32, (CHUNK, CHUNK), 0)
    col = lax.broadcasted_iota(i32, (CHUNK, CHUNK), 1)
    causal = row >= col
    tri = causal.astype(f32)
    for n in range(n_chunks):
        sl = pl.ds(n * CHUNK, CHUNK)
        z = f_ref[sl, :]
        log_f = jnp.log(lb + (1.0 - lb) * jax.nn.sigmoid(z))
        kk = (1.0 - lb) * jax.nn.sigmoid(-z)
        bc = jnp.dot(tri, log_f, precision=HIGHEST, preferred_element_type=f32)
        b_mid = bc[CHUNK // 2 - 1:CHUNK // 2, :]
        b_last = bc[CHUNK - 1:CHUNK, :]
        q = q_ref[sl, :]
        v = i_ref[sl, :]
        g = g_ref[sl, :]
        q_in = (q * jnp.exp(bc - b_mid)).astype(bf16)
        k_in = (kk * jnp.exp(b_mid - bc)).astype(bf16)
        k_dec = kk * jnp.exp(b_last - bc)
        q_dec = (q * jnp.exp(bc)).astype(bf16)
        decay = jnp.exp(b_last)
        outs = []
        for h in range(B_HEADS):
            hs = slice(h * B_KEY_DIM, (h + 1) * B_KEY_DIM)
            attn = lax.dot_general(q_in[:, hs], k_in[:, hs], _NT, preferred_element_type=f32)
            attn = jnp.where(causal, attn, 0.0)
            vh = v[:, hs]
            o = jnp.dot(attn.astype(bf16), vh.astype(bf16), preferred_element_type=f32)
            st = st_ref[h]
            o = o + lax.dot_general(q_dec[:, hs], st.astype(bf16), _NT, preferred_element_type=f32)
            d_st = jnp.dot(vh.T.astype(bf16), k_dec[:, hs].astype(bf16), preferred_element_type=f32)
            st_ref[h] = decay[:, hs] * st + d_st
            ms = jnp.mean(o * o, axis=-1, keepdims=True)
            gh = g[:, hs]
            outs.append(o * lax.rsqrt(ms + RMS_EPS) * ng * gh * jax.nn.sigmoid(gh))
        o_ref[sl, :] = jnp.concatenate(outs, axis=1)


def _hgrn(proj, lb, norm_g, B, S, tc=256):
    nc = S // tc
    blk = lambda c: pl.BlockSpec((tc, B_K), lambda b, s, c=c: (b * nc + s, c))
    return pl.pallas_call(
        functools.partial(_hgrn_body, n_chunks=tc // CHUNK),
        grid=(B, nc),
        in_specs=[blk(COL_QB // B_K), blk(COL_FB // B_K), blk(COL_IB // B_K), blk(COL_GB // B_K),
                  pl.BlockSpec((1, B_K), lambda b, s: (0, 0)),
                  pl.BlockSpec((1, B_VAL_DIM), lambda b, s: (0, 0))],
        out_specs=pl.BlockSpec((tc, B_V), lambda b, s: (b * nc + s, 0)),
        out_shape=jax.ShapeDtypeStruct((B * S, B_V), f32),
        scratch_shapes=[pltpu.VMEM((B_HEADS, B_VAL_DIM, B_KEY_DIM), f32)],
        compiler_params=_params("arbitrary", "arbitrary"),
        name="hgrn",
    )(proj, proj, proj, proj, lb, norm_g)


def _dsa_body(qa_ref, qi_ref, kwq_ref, ckv_ref, kw_ref, g_ref, wuk_ref, wuv_ref, bt_ref, bfar_ref, o_ref,
              ckvn_ref, skey_ref, s_ref, mt_ref, lt_ref, acc_ref, qlat_ref, qih_ref, wq_ref, thr_ref, ilim_ref,
              *, S, n_sel):
    j = pl.program_id(1)
    QB = Q_BLOCK
    n_kb = S // QB
    scale = A_HEAD_DIM ** -0.5
    rowi = lax.broadcasted_iota(i32, (QB, QB), 0)
    coli = lax.broadcasted_iota(i32, (QB, QB), 1)

    @pl.when(j == 0)
    def _():
        def norm_blk(kb, c):
            sl = pl.ds(pl.multiple_of(kb * QB, QB), QB)
            cb = ckv_ref[sl, :]
            ms = jnp.mean(cb * cb, axis=-1, keepdims=True)
            ckvn_ref[sl, :] = (cb * lax.rsqrt(ms + RMS_EPS) * g_ref[...]).astype(bf16)
            return c
        lax.fori_loop(0, n_kb, norm_blk, 0)

    qa = qa_ref[...]
    qi = qi_ref[...]
    wq = kwq_ref[:, IDX_DIM:IDX_DIM + IDX_HEADS] * ((IDX_HEADS * IDX_DIM) ** -0.5)
    for h in range(A_HEADS):
        qh = qa[:, h * A_HEAD_DIM:(h + 1) * A_HEAD_DIM].astype(bf16)
        qlat_ref[h * QB:(h + 1) * QB, :] = jnp.dot(qh, wuk_ref[h], preferred_element_type=f32).astype(bf16)
    for h in range(IDX_HEADS):
        qih_ref[h * QB:(h + 1) * QB, :] = qi[:, h * IDX_DIM:(h + 1) * IDX_DIM].astype(bf16)
        wq_ref[h] = jnp.broadcast_to(wq[:, h:h + 1], (QB, QB))

    def index_block(kb, is_current):
        sl = pl.ds(pl.multiple_of(kb * QB, QB), QB)
        kblk = kw_ref[sl, :][:, :IDX_DIM].astype(bf16)
        lg_all = lax.dot_general(qih_ref[...], kblk, _NT, preferred_element_type=f32)
        sc = jnp.zeros((QB, QB), f32)
        for h in range(IDX_HEADS):
            sc = sc + wq_ref[h] * jnp.maximum(lg_all[h * QB:(h + 1) * QB], 0.0)
        bits = lax.bitcast_convert_type(sc, i32)
        key = bits ^ ((bits >> 31) & jnp.int32(0x7FFFFFFF))
        if is_current:
            key = jnp.where((rowi < CHUNK) & (coli >= CHUNK), jnp.int32(INT_MIN), key)
        skey_ref[kb] = key

    def idx_loop(kb, c):
        index_block(kb, False)
        return c
    lax.fori_loop(0, j, idx_loop, 0)
    index_block(j, True)

    def count_where(pred_fn):
        def body(kb, acc):
            return acc + jnp.where(pred_fn(kb, skey_ref[kb]), 1.0, 0.0)
        acc = lax.fori_loop(0, j + 1, body, jnp.zeros((QB, QB), f32))
        return jnp.sum(acc, axis=1, keepdims=True)

    def bcast(col):
        return jnp.broadcast_to(col, (QB, QB))

    thr_ref[...] = jnp.full((QB, 1), INT_MIN, i32)
    ilim_ref[...] = jnp.full((QB, 1), S, i32)
    kf = float(n_sel)

    @pl.when((j + 1) * QB > n_sel)
    def _():
        zero = jnp.zeros((QB, 1), i32)
        c0 = count_where(lambda kb, key: key >= 0)
        n_all = jnp.zeros((QB, 1), f32) + ((j + 1) * QB).astype(f32)
        thr = jnp.where(c0 >= kf, zero, jnp.int32(INT_MIN))
        c_thr = jnp.where(c0 >= kf, c0, n_all)

        def bit_step(i, carry):
            thr, c_thr = carry
            cand = thr | jnp.left_shift(jnp.int32(1), 30 - i)
            cb = bcast(cand)
            c = count_where(lambda kb, key: key >= cb)
            ok = c >= kf
            return jnp.where(ok, cand, thr), jnp.where(ok, c, c_thr)
        thr, c_thr = lax.fori_loop(0, 31, bit_step, (thr, c_thr))
        thr_ref[...] = thr

        @pl.when(jnp.max(c_thr) > kf)
        def _():
            tb = bcast(thr)
            need = kf - count_where(lambda kb, key: key > tb)
            n_bits = int(math.ceil(math.log2(S + 1)))

            def lim_step(i, lim):
                cand = lim + jnp.left_shift(jnp.int32(1), n_bits - 1 - i)
                cb = bcast(cand)
                c = count_where(lambda kb, key: (key == tb) & (kb * QB + coli < cb))
                return jnp.where(c <= need, cand, lim)
            ilim_ref[...] = lax.fori_loop(0, n_bits, lim_step, zero)

    mt_ref[...] = jnp.full(mt_ref.shape, NEG_BIG, f32)
    thr_b = bcast(thr_ref[...])
    lim_b = bcast(ilim_ref[...])

    def score_block(kb, bias_of_head):
        sl = pl.ds(pl.multiple_of(kb * QB, QB), QB)
        key = skey_ref[kb]
        sel = ((key > thr_b) | ((key == thr_b) & (kb * QB + coli < lim_b))) & (key > jnp.int32(INT_MIN))
        s_all = lax.dot_general(qlat_ref[...], ckvn_ref[sl, :], _NT, preferred_element_type=f32)
        for h in range(A_HEADS):
            s = s_all[h * QB:(h + 1) * QB] * scale + bias_of_head(h)
            s = jnp.where(sel, s, NEG_BIG)
            s_ref[kb, h] = s
            mt_ref[h] = jnp.maximum(mt_ref[h], s)

    def far_loop(kb, c):
        score_block(kb, lambda h: bfar_ref[h:h + 1, :])
        return c
    lax.fori_loop(0, jnp.maximum(j - 1, 0), far_loop, 0)

    @pl.when(j >= 1)
    def _():
        score_block(j - 1, lambda h: bt_ref[h, :, 0:QB])
    score_block(j, lambda h: bt_ref[h, :, QB:2 * QB])

    for h in range(A_HEADS):
        mt_ref[h] = bcast(jnp.max(mt_ref[h], axis=1, keepdims=True))
    lt_ref[...] = jnp.zeros(lt_ref.shape, f32)
    acc_ref[...] = jnp.zeros(acc_ref.shape, f32)

    def pv_block(kb, c):
        sl = pl.ds(pl.multiple_of(kb * QB, QB), QB)
        ps = []
        for h in range(A_HEADS):
            p = jnp.exp(s_ref[kb, h] - mt_ref[h])
            lt_ref[h] = lt_ref[h] + p
            ps.append(p.astype(bf16))
        acc_ref[...] = acc_ref[...] + jnp.dot(jnp.concatenate(ps, axis=0), ckvn_ref[sl, :],
                                              preferred_element_type=f32)
        return c
    lax.fori_loop(0, j + 1, pv_block, 0)

    outs = []
    for h in range(A_HEADS):
        o_lat = acc_ref[h * QB:(h + 1) * QB, :] / jnp.sum(lt_ref[h], axis=1, keepdims=True)
        outs.append(jnp.dot(o_lat.astype(bf16), wuv_ref[h], preferred_element_type=f32))
    o_ref[...] = jnp.concatenate(outs, axis=1)


def _t5_bucket(rel):
    half = REL_BUCKETS // 2
    max_exact = half // 2
    base = jnp.where(rel > 0, half, 0)
    n = jnp.abs(rel)
    nf = jnp.maximum(n, 1).astype(f32)
    large = max_exact + (jnp.log(nf / max_exact) / math.log(REL_MAX_DIST / max_exact) * (half - max_exact)).astype(i32)
    large = jnp.minimum(large, half - 1)
    return base + jnp.where(n < max_exact, n, large)


def _dsa(proj, kv_norm_g, w_uk, w_uv, rel_bias, B, S):
    QB = Q_BLOCK
    nq = S // QB
    n_sel = min(TOPK_MAX, S // 4)
    assert REL_MAX_DIST <= QB
    r = jnp.arange(QB, dtype=i32)[:, None]
    c = jnp.arange(2 * QB, dtype=i32)[None, :]
    bias_tile = jnp.moveaxis(rel_bias[_t5_bucket(c - QB - r)], -1, 0).astype(f32)
    bias_far = jnp.broadcast_to(rel_bias[_t5_bucket(jnp.int32(-QB - 1))][:, None], (A_HEADS, LANES)).astype(f32)
    wuk_t = jnp.transpose(w_uk, (1, 2, 0)).astype(bf16)
    wuv_h = jnp.transpose(w_uv, (1, 0, 2)).astype(bf16)
    qblk = lambda width, cb: pl.BlockSpec((QB, width), lambda b, j, cb=cb: (b * nq + j, cb))
    seq = lambda cb: pl.BlockSpec((S, LANES), lambda b, j, cb=cb: (b, cb))
    full = lambda a: pl.BlockSpec(a.shape, lambda b, j, n=a.ndim: (0,) * n)
    return pl.pallas_call(
        functools.partial(_dsa_body, S=S, n_sel=n_sel),
        grid=(B, nq),
        in_specs=[qblk(A_Q, COL_QA // A_Q), qblk(A_Q, COL_QI // A_Q), qblk(LANES, COL_KIWI // LANES),
                  seq(COL_CKV // LANES), seq(COL_KIWI // LANES),
                  full(kv_norm_g), full(wuk_t), full(wuv_h), full(bias_tile), full(bias_far)],
        out_specs=pl.BlockSpec((QB, A_Q), lambda b, j: (b * nq + j, 0)),
        out_shape=jax.ShapeDtypeStruct((B * S, A_Q), f32),
        scratch_shapes=[
            pltpu.VMEM((S, A_LATENT), bf16),
            pltpu.VMEM((S // QB, QB, QB), i32),
            pltpu.VMEM((S // QB, A_HEADS, QB, QB), f32),
            pltpu.VMEM((A_HEADS, QB, QB), f32),
            pltpu.VMEM((A_HEADS, QB, QB), f32),
            pltpu.VMEM((A_HEADS * QB, A_LATENT), f32),
            pltpu.VMEM((A_HEADS * QB, A_LATENT), bf16),
            pltpu.VMEM((IDX_HEADS * QB, IDX_DIM), bf16),
            pltpu.VMEM((IDX_HEADS, QB, QB), f32),
            pltpu.VMEM((QB, 1), i32),
            pltpu.VMEM((QB, 1), i32),
        ],
        compiler_params=_params("arbitrary", "arbitrary"),
        name="dsa",
    )(proj, proj, proj, proj, proj, kv_norm_g, wuk_t, wuv_h, bias_tile, bias_far)


def _paired(n, fn, init=None):
    def pair(i, c):
        return fn(2 * i + 1, fn(2 * i, c))
    c = lax.fori_loop(0, lax.shift_right_logical(n, 1), pair, init)
    return lax.cond((n & 1) == 1, lambda c: fn(n - 1, c), lambda c: c, c)


def _dsa_t_body(qa_ref, qi_ref, kwq_ref, ckv_ref, kw_ref, g_ref, wuk_ref, wuvt_ref, btt_ref, bfar_ref, o_ref,
                ckvn_ref, ckvnt_ref, skey_ref, s_ref, mt_ref, lt_ref, acc_ref, qlat_ref, qih_ref, wqt_ref,
                mrow_ref, thr_ref, ilim_ref, *, S, n_sel):
    j = pl.program_id(1)
    QB = Q_BLOCK
    n_kb = S // QB
    scale = A_HEAD_DIM ** -0.5
    keyi = lax.broadcasted_iota(i32, (QB, QB), 0)
    qryi = lax.broadcasted_iota(i32, (QB, QB), 1)

    def rows(kb):
        return pl.ds(pl.multiple_of(kb * QB, QB), QB)

    @pl.when(j == 0)
    def _():
        def norm_blk(kb, c):
            cb = ckv_ref[rows(kb), :]
            ms = jnp.mean(cb * cb, axis=-1, keepdims=True)
            cn = cb * lax.rsqrt(ms + RMS_EPS) * g_ref[...]
            ckvn_ref[rows(kb), :] = cn.astype(bf16)
            ckvnt_ref[kb] = cn.T.astype(bf16)
            return c
        lax.fori_loop(0, n_kb, norm_blk, 0)

    qa = qa_ref[...]
    qi = qi_ref[...]
    wqt_ref[...] = kwq_ref[...].T[IDX_DIM:IDX_DIM + IDX_HEADS, :] * ((IDX_HEADS * IDX_DIM) ** -0.5)
    for h in range(A_HEADS):
        qh = qa[:, h * A_HEAD_DIM:(h + 1) * A_HEAD_DIM].astype(bf16)
        qlat_ref[h * QB:(h + 1) * QB, :] = jnp.dot(qh, wuk_ref[h], preferred_element_type=f32).astype(bf16)
    for h in range(IDX_HEADS):
        qih_ref[h * QB:(h + 1) * QB, :] = qi[:, h * IDX_DIM:(h + 1) * IDX_DIM].astype(bf16)

    def index_block(kb, is_current):
        kblk = kw_ref[rows(kb), :][:, :IDX_DIM].astype(bf16)
        lg_all = lax.dot_general(kblk, qih_ref[...], _NT, preferred_element_type=f32)
        sc = jnp.zeros((QB, QB), f32)
        for h in range(IDX_HEADS):
            sc = sc + wqt_ref[h:h + 1, :] * jnp.maximum(lg_all[:, h * QB:(h + 1) * QB], 0.0)
        bits = lax.bitcast_convert_type(sc, i32)
        key = bits ^ ((bits >> 31) & jnp.int32(0x7FFFFFFF))
        if is_current:
            key = jnp.where((qryi < CHUNK) & (keyi >= CHUNK), jnp.int32(INT_MIN), key)
        skey_ref[kb] = key

    def idx_loop(kb, c):
        index_block(kb, False)
        return c
    _paired(j, idx_loop)
    index_block(j, True)

    def count_where(pred_fn):
        def body(kb, acc):
            return acc + jnp.where(pred_fn(kb, skey_ref[kb]), 1.0, 0.0)
        acc = _paired(j + 1, body, jnp.zeros((QB, QB), f32))
        return jnp.sum(acc, axis=0, keepdims=True)

    thr_ref[...] = jnp.full((1, QB), INT_MIN, i32)
    ilim_ref[...] = jnp.full((1, QB), S, i32)
    kf = float(n_sel)

    @pl.when((j + 1) * QB > n_sel)
    def _():
        zero = jnp.zeros((1, QB), i32)
        c0 = count_where(lambda kb, key: key >= 0)
        n_all = jnp.zeros((1, QB), f32) + ((j + 1) * QB).astype(f32)
        thr = jnp.where(c0 >= kf, zero, jnp.int32(INT_MIN))
        c_thr = jnp.where(c0 >= kf, c0, n_all)

        def bit_step(i, carry):
            thr, c_thr = carry
            cand = thr | jnp.left_shift(jnp.int32(1), 30 - i)
            c = count_where(lambda kb, key: key >= cand)
            ok = c >= kf
            return jnp.where(ok, cand, thr), jnp.where(ok, c, c_thr)
        thr, c_thr = lax.fori_loop(0, 31, bit_step, (thr, c_thr))
        thr_ref[...] = thr

        @pl.when(jnp.max(c_thr) > kf)
        def _():
            need = kf - count_where(lambda kb, key: key > thr)
            n_bits = int(math.ceil(math.log2(S + 1)))

            def lim_step(i, lim):
                cand = lim + jnp.left_shift(jnp.int32(1), n_bits - 1 - i)
                c = count_where(lambda kb, key: (key == thr) & (kb * QB + keyi < cand))
                return jnp.where(c <= need, cand, lim)
            ilim_ref[...] = lax.fori_loop(0, n_bits, lim_step, zero)

    mt_ref[...] = jnp.full(mt_ref.shape, NEG_BIG, f32)
    thr = thr_ref[...]
    lim = ilim_ref[...]

    def score_block(kb, bias_of_head):
        key = skey_ref[kb]
        sel = ((key > thr) | ((key == thr) & (kb * QB + keyi < lim))) & (key > jnp.int32(INT_MIN))
        s_all = lax.dot_general(ckvn_ref[rows(kb), :], qlat_ref[...], _NT, preferred_element_type=f32)
        for h in range(A_HEADS):
            s = s_all[:, h * QB:(h + 1) * QB] * scale + bias_of_head(h)
            s = jnp.where(sel, s, NEG_BIG)
            s_ref[kb, h] = s
            mt_ref[h] = jnp.maximum(mt_ref[h], s)

    def far_loop(kb, c):
        score_block(kb, lambda h: bfar_ref[h:h + 1, :])
        return c
    _paired(jnp.maximum(j - 1, 0), far_loop)

    @pl.when(j >= 1)
    def _():
        score_block(j - 1, lambda h: btt_ref[h, 0:QB, :])
    score_block(j, lambda h: btt_ref[h, QB:2 * QB, :])

    for h in range(A_HEADS):
        mrow_ref[h:h + 1, :] = jnp.max(mt_ref[h], axis=0, keepdims=True)
    lt_ref[...] = jnp.zeros(lt_ref.shape, f32)
    acc_ref[...] = jnp.zeros(acc_ref.shape, f32)

    def pv_block(kb, c):
        ps = []
        for h in range(A_HEADS):
            p = jnp.exp(s_ref[kb, h] - mrow_ref[h:h + 1, :])
            lt_ref[h] = lt_ref[h] + p
            ps.append(p.astype(bf16))
        acc_ref[...] = acc_ref[...] + jnp.dot(ckvnt_ref[kb], jnp.concatenate(ps, axis=1),
                                              preferred_element_type=f32)
        return c
    _paired(j + 1, pv_block)

    outs = []
    for h in range(A_HEADS):
        o_lat_t = acc_ref[:, h * QB:(h + 1) * QB] / jnp.sum(lt_ref[h], axis=0, keepdims=True)
        outs.append(jnp.dot(wuvt_ref[h], o_lat_t.astype(bf16), preferred_element_type=f32))
    o_ref[...] = jnp.concatenate(outs, axis=0).T


def _dsa_t(proj, kv_norm_g, w_uk, w_uv, rel_bias, B, S):
    QB = Q_BLOCK
    nq = S // QB
    n_sel = min(TOPK_MAX, S // 4)
    assert REL_MAX_DIST <= QB
    c = jnp.arange(2 * QB, dtype=i32)[:, None]
    r = jnp.arange(QB, dtype=i32)[None, :]
    bias_tile_t = jnp.moveaxis(rel_bias[_t5_bucket(c - QB - r)], -1, 0).astype(f32)
    bias_far = jnp.broadcast_to(rel_bias[_t5_bucket(jnp.int32(-QB - 1))][:, None], (A_HEADS, LANES)).astype(f32)
    wuk_t = jnp.transpose(w_uk, (1, 2, 0)).astype(bf16)
    wuv_t = jnp.transpose(w_uv, (1, 2, 0)).astype(bf16)
    qblk = lambda width, cb: pl.BlockSpec((QB, width), lambda b, j, cb=cb: (b * nq + j, cb))
    seq = lambda cb: pl.BlockSpec((S, LANES), lambda b, j, cb=cb: (b, cb))
    full = lambda a: pl.BlockSpec(a.shape, lambda b, j, n=a.ndim: (0,) * n)
    return pl.pallas_call(
        functools.partial(_dsa_t_body, S=S, n_sel=n_sel),
        grid=(B, nq),
        in_specs=[qblk(A_Q, COL_QA // A_Q), qblk(A_Q, COL_QI // A_Q), qblk(LANES, COL_KIWI // LANES),
                  seq(COL_CKV // LANES), seq(COL_KIWI // LANES),
                  full(kv_norm_g), full(wuk_t), full(wuv_t), full(bias_tile_t), full(bias_far)],
        out_specs=pl.BlockSpec((QB, A_Q), lambda b, j: (b * nq + j, 0)),
        out_shape=jax.ShapeDtypeStruct((B * S, A_Q), f32),
        scratch_shapes=[
            pltpu.VMEM((S, A_LATENT), bf16),
            pltpu.VMEM((S // QB, A_LATENT, QB), bf16),
            pltpu.VMEM((S // QB, QB, QB), i32),
            pltpu.VMEM((S // QB, A_HEADS, QB, QB), f32),
            pltpu.VMEM((A_HEADS, QB, QB), f32),
            pltpu.VMEM((A_HEADS, QB, QB), f32),
            pltpu.VMEM((A_LATENT, A_HEADS * QB), f32),
            pltpu.VMEM((A_HEADS * QB, A_LATENT), bf16),
            pltpu.VMEM((IDX_HEADS * QB, IDX_DIM), bf16),
            pltpu.VMEM((IDX_HEADS, QB), f32),
            pltpu.VMEM((A_HEADS, QB), f32),
            pltpu.VMEM((1, QB), i32),
            pltpu.VMEM((1, QB), i32),
        ],
        compiler_params=_params("arbitrary", "arbitrary"),
        name="dsa",
    )(proj, proj, proj, proj, proj, kv_norm_g, wuk_t, wuv_t, bias_tile_t, bias_far)


def _layer_norm(y, g, b):
    mu = jnp.mean(y, axis=-1, keepdims=True)
    var = jnp.mean(jnp.square(y - mu), axis=-1, keepdims=True)
    return (y - mu) * lax.rsqrt(var + LN_EPS) * g + b


def _merge_body(oa_ref, ob_ref, ga_ref, gb_ref, x_ref, wa_ref, wb_ref, wo_ref, g_ref, b_ref, o_ref, ot_ref):
    ya = jnp.dot(oa_ref[...].astype(bf16), wa_ref[...], preferred_element_type=f32)
    yb = jnp.dot(ob_ref[...].astype(bf16), wb_ref[...], preferred_element_type=f32)
    merged = jax.nn.sigmoid(ga_ref[...]) * ya + jax.nn.sigmoid(gb_ref[...]) * yb
    mix = jnp.dot(merged.astype(bf16), wo_ref[...], preferred_element_type=f32)
    y = _layer_norm(ALPHA * x_ref[...] + mix, g_ref[...], b_ref[...])
    o_ref[...] = y
    for q in range(D_MODEL // LANES):
        ot_ref[:, q, :] = y[:, q * LANES:(q + 1) * LANES]


def _merge(oa, ob, proj, x2, wa, wb, wo, g, b, tm=256):
    T = x2.shape[0]
    full = lambda a: pl.BlockSpec(a.shape, lambda i: (0, 0))
    return pl.pallas_call(
        _merge_body,
        grid=(T // tm,),
        in_specs=[pl.BlockSpec((tm, A_Q), lambda i: (i, 0)), pl.BlockSpec((tm, B_V), lambda i: (i, 0)),
                  pl.BlockSpec((tm, D_MODEL), lambda i: (i, COL_GATE_A // D_MODEL)),
                  pl.BlockSpec((tm, D_MODEL), lambda i: (i, COL_GATE_B // D_MODEL)),
                  pl.BlockSpec((tm, D_MODEL), lambda i: (i, 0)),
                  full(wa), full(wb), full(wo), full(g), full(b)],
        out_specs=[pl.BlockSpec((tm, D_MODEL), lambda i: (i, 0)),
                   pl.BlockSpec((tm, D_MODEL // LANES, LANES), lambda i: (i, 0, 0))],
        out_shape=[jax.ShapeDtypeStruct((T, D_MODEL), f32),
                   jax.ShapeDtypeStruct((T, D_MODEL // LANES, LANES), f32)],
        compiler_params=_params("arbitrary"),
        name="merge",
    )(oa, ob, proj, proj, x2, wa, wb, wo, g, b)


def _route_body(x_ref, wpq_ref, sk1_ref, sk2_ref, ids_ref, gate_ref,
                v1_ref, i1_ref, v2_ref, i2_ref, ts_ref, ti_ref, idt_ref, gt_ref):
    tb = x_ref.shape[0]
    K = PEER_TOPK
    q = jnp.dot(x_ref[...].astype(bf16), wpq_ref[...], preferred_element_type=f32)
    kiota = lax.broadcasted_iota(i32, (PEER_KEYS, tb), 0).astype(f32)
    r8 = lax.broadcasted_iota(i32, (8, tb), 0).astype(f32)
    r16 = lax.broadcasted_iota(i32, (16, tb), 0).astype(f32)
    flat = jnp.concatenate([r16] + [K * a + r8 for a in range(1, 8)] + [K * (8 + r8)], axis=0)
    half = PEER_QDIM // 2
    for h in range(PEER_HEADS):
        for part, (sk_ref, v_ref, i_ref) in enumerate(((sk1_ref, v1_ref, i1_ref), (sk2_ref, v2_ref, i2_ref))):
            off = h * PEER_QDIM + part * half
            qh = q[:, off:off + half].astype(bf16)
            s = lax.dot_general(sk_ref[...], qh, _NT, preferred_element_type=f32)
            for r in range(K):
                m = jnp.max(s, axis=0, keepdims=True)
                ix = jnp.min(jnp.where(s == m, kiota, float(PEER_KEYS)), axis=0, keepdims=True)
                v_ref[r:r + 1, :] = m
                i_ref[r:r + 1, :] = ix
                s = jnp.where(kiota == ix, -jnp.inf, s)
        v1, i1, v2, i2 = v1_ref[...], i1_ref[...], v2_ref[...], i2_ref[...]
        cand = [v1[0:1] + v2, ] + [v1[a:a + 1] + v2[0:8] for a in range(1, 8)] + [v1[8:16] + v2[0:1]]
        cid = [i1[0:1] * PEER_KEYS + i2, ] + [i1[a:a + 1] * PEER_KEYS + i2[0:8] for a in range(1, 8)] \
            + [i1[8:16] * PEER_KEYS + i2[0:1]]
        cand = jnp.concatenate(cand, axis=0)
        cid = jnp.concatenate(cid, axis=0)
        for r in range(K):
            m = jnp.max(cand, axis=0, keepdims=True)
            fsel = jnp.min(jnp.where(cand == m, flat, float(K * K)), axis=0, keepdims=True)
            hit = flat == fsel
            ts_ref[r:r + 1, :] = m
            ti_ref[r:r + 1, :] = jnp.sum(jnp.where(hit, cid, 0.0), axis=0, keepdims=True)
            cand = jnp.where(hit, -jnp.inf, cand)
        ts = ts_ref[...]
        e = jnp.exp(ts - jnp.max(ts, axis=0, keepdims=True))
        gt_ref[h * K:(h + 1) * K, :] = e / jnp.sum(e, axis=0, keepdims=True)
        idt_ref[h * K:(h + 1) * K, :] = (ti_ref[...] * ROWS_PER_EXPERT).astype(i32)
    gate_ref[...] = gt_ref[...].T
    ids_ref[...] = lax.bitcast_convert_type(lax.bitcast_convert_type(idt_ref[...], f32).T, i32)


def _route(x1, wpq, sk1, sk2, tb=256):
    T = x1.shape[0]
    HK = PEER_HEADS * PEER_TOPK
    full = lambda a: pl.BlockSpec(a.shape, lambda i: (0, 0))
    return pl.pallas_call(
        _route_body,
        grid=(T // tb,),
        in_specs=[pl.BlockSpec((tb, D_MODEL), lambda i: (i, 0)), full(wpq), full(sk1), full(sk2)],
        out_specs=[pl.BlockSpec((tb, HK), lambda i: (i, 0)), pl.BlockSpec((tb, HK), lambda i: (i, 0))],
        out_shape=[jax.ShapeDtypeStruct((T, HK), i32), jax.ShapeDtypeStruct((T, HK), f32)],
        scratch_shapes=[pltpu.VMEM((PEER_TOPK, tb), f32)] * 6 + [
                        pltpu.VMEM((HK, tb), i32), pltpu.VMEM((HK, tb), f32)],
        compiler_params=_params("arbitrary"),
        name="route",
    )(x1, wpq, sk1, sk2)


SLOTS = PEER_HEADS * PEER_TOPK
G_ROWS = SLOTS * ROWS_PER_EXPERT
G_ROWS16 = 2 * G_ROWS
TOKENS_PER_STEP_U = 4
TOKENS_PER_STEP_V = 8


def _pack_table(t):
    E, D = t.shape
    tb = lax.bitcast_convert_type(t.astype(jnp.bfloat16), jnp.uint16).astype(jnp.uint32)
    tb = tb.reshape(E, ROWS_PER_EXPERT, 2, LANES)
    return lax.bitcast_convert_type((tb[:, :, 1] << 16) | tb[:, :, 0], i32).reshape(E * ROWS_PER_EXPERT, LANES)


def _gather_rows(rows_ref, t, tab_ref, g_ref):
    R = ROWS_PER_EXPERT
    for k in range(SLOTS):
        r0 = rows_ref[t, k]
        g_ref[R * k:R * (k + 1), :] = tab_ref[pl.ds(pl.multiple_of(r0, R), R), :]


def _split2(a):
    hi = a.astype(bf16)
    return jnp.concatenate([hi, (a - hi.astype(f32)).astype(bf16)], axis=0)


def _dot_select(a, sel_ref):
    hi = a.astype(bf16)
    lo = (a - hi.astype(f32)).astype(bf16)
    return (jnp.dot(hi, sel_ref[...], preferred_element_type=f32)
            + jnp.dot(lo, sel_ref[...], preferred_element_type=f32))


def _token_pipeline(tb, rows_ref, tab_ref, g_scr, issue, finish):
    per_step = g_scr.shape[0]
    for u in range(per_step):
        _gather_rows(rows_ref, u, tab_ref, g_scr.at[u])

    def step(i, c):
        for u in range(per_step):
            t = i * per_step + u
            z = issue(t, pltpu.bitcast(g_scr[u], jnp.bfloat16))
            _gather_rows(rows_ref, jnp.minimum(t + per_step, tb - 1), tab_ref, g_scr.at[u])
            finish(t, z)
        return c
    lax.fori_loop(0, tb // per_step, step, 0)


def _peer_u_body(rows_ref, x_ref, gate_ref, tab_ref, mask_ref, fold_ref, w_ref, g_scr, r_scr):
    tb = x_ref.shape[0]

    def issue(t, gb):
        return lax.dot_general(_split2(x_ref[t]), gb, _NT, preferred_element_type=f32)

    def finish(t, z):
        r_scr[pl.ds(t, 1), :] = jnp.sum((z[0:8] + z[8:16]) * mask_ref[...], axis=0, keepdims=True)

    _token_pipeline(tb, rows_ref, tab_ref, g_scr, issue, finish)
    h = _dot_select(r_scr[...], fold_ref)
    w_ref[...] = gate_ref[...] * (0.5 * h * (1.0 + lax.erf(h * (2.0 ** -0.5))))


def _chunk_mask():
    return (np.arange(G_ROWS16)[None, :] % 8 == np.arange(8)[:, None]).astype(np.float32)


def _peer_u(rows, x1r, gate, tab, tb=256):
    T = rows.shape[0]
    fold = (np.arange(G_ROWS16)[:, None] // 8 == np.arange(SLOTS)[None, :]).astype(np.float32)
    full = lambda a: pl.BlockSpec(a.shape, lambda i: (0, 0))
    return pl.pallas_call(
        _peer_u_body,
        grid=(T // tb,),
        in_specs=[pl.BlockSpec((tb, SLOTS), lambda i: (i, 0), memory_space=pltpu.SMEM),
                  pl.BlockSpec((tb, 8, LANES), lambda i: (i, 0, 0)),
                  pl.BlockSpec((tb, SLOTS), lambda i: (i, 0)),
                  full(tab), pl.BlockSpec((8, G_ROWS16), lambda i: (0, 0)),
                  pl.BlockSpec((G_ROWS16, SLOTS), lambda i: (0, 0))],
        out_specs=pl.BlockSpec((tb, SLOTS), lambda i: (i, 0)),
        out_shape=jax.ShapeDtypeStruct((T, SLOTS), f32),
        scratch_shapes=[pltpu.VMEM((TOKENS_PER_STEP_U, G_ROWS, LANES), i32), pltpu.VMEM((tb, G_ROWS16), f32)],
        compiler_params=_params("arbitrary", vmem=TABLE_VMEM_LIMIT),
        name="peer_u",
    )(rows, x1r, gate, tab, jnp.asarray(_chunk_mask()), jnp.asarray(fold, dtype=jnp.bfloat16))


def _peer_v_body(rows_ref, w_ref, x_ref, tab_ref, exp_ref, mask_ref, g_ref, b_ref, o_ref, g_scr, wexp_scr, y_scr):
    tb = x_ref.shape[0]
    wexp_scr[...] = _dot_select(w_ref[...], exp_ref)

    def issue(t, gb):
        return jnp.dot(_split2(mask_ref[...] * wexp_scr[pl.ds(t, 1), :]), gb, preferred_element_type=f32)

    def finish(t, z):
        y_scr[t] = z[0:8] + z[8:16]

    _token_pipeline(tb, rows_ref, tab_ref, g_scr, issue, finish)
    y = ALPHA * x_ref[...] + y_scr[...]
    n_el = float(D_MODEL)
    mu = jnp.sum(y, axis=(1, 2), keepdims=True) / n_el
    yc = y - mu
    var = jnp.sum(yc * yc, axis=(1, 2), keepdims=True) / n_el
    y_scr[...] = yc * lax.rsqrt(var + LN_EPS) * g_ref[...] + b_ref[...]
    for q in range(D_MODEL // LANES):
        o_ref[:, q * LANES:(q + 1) * LANES] = y_scr[:, q, :]


def _peer_v(rows, w, x1r, tab, g, b, tb=256):
    T = rows.shape[0]
    expand = (np.arange(SLOTS)[:, None] == np.arange(G_ROWS16)[None, :] // 8).astype(np.float32)
    full = lambda a: pl.BlockSpec(a.shape, lambda i: (0, 0))
    return pl.pallas_call(
        _peer_v_body,
        grid=(T // tb,),
        in_specs=[pl.BlockSpec((tb, SLOTS), lambda i: (i, 0), memory_space=pltpu.SMEM),
                  pl.BlockSpec((tb, SLOTS), lambda i: (i, 0)),
                  pl.BlockSpec((tb, 8, LANES), lambda i: (i, 0, 0)),
                  full(tab), pl.BlockSpec((SLOTS, G_ROWS16), lambda i: (0, 0)),
                  pl.BlockSpec((8, G_ROWS16), lambda i: (0, 0)),
                  pl.BlockSpec((8, LANES), lambda i: (0, 0)),
                  pl.BlockSpec((8, LANES), lambda i: (0, 0))],
        out_specs=pl.BlockSpec((tb, D_MODEL), lambda i: (i, 0)),
        out_shape=jax.ShapeDtypeStruct((T, D_MODEL), f32),
        scratch_shapes=[pltpu.VMEM((TOKENS_PER_STEP_V, G_ROWS, LANES), i32), pltpu.VMEM((tb, G_ROWS16), f32),
                        pltpu.VMEM((tb, 8, LANES), f32)],
        compiler_params=_params("arbitrary", vmem=TABLE_VMEM_LIMIT),
        name="peer_v",
    )(rows, w, x1r, tab, jnp.asarray(expand, dtype=jnp.bfloat16), jnp.asarray(_chunk_mask()), g, b)


def _regroup_w_in(w):
    offs = np.cumsum(np.array(IN_SPLITS))[:-1].tolist()
    qa, ckv, qi, ki, wi, qb, fb, ib, gb, gate_a, gate_b = jnp.split(w, offs, axis=-1)
    pad = jnp.zeros((w.shape[0], LANES - IDX_DIM - IDX_HEADS), w.dtype)
    out = jnp.concatenate([qa, qi, qb, fb, ib, gb, gate_a, gate_b, ckv, ki, wi, pad], axis=-1)
    assert out.shape[-1] == PROJ_COLS
    return out.astype(bf16)


def kernel(x, w_in, kv_norm_g, w_uk, w_uv, rel_bias, lb_params, b_norm_g, w_br_a, w_br_b, w_o, ln1_g, ln1_b,
           w_pq, sub_keys1, sub_keys2, u_table, v_table, ln2_g, ln2_b):
    B, S, D = x.shape
    T = B * S
    lower_bounds = jnp.cumsum(jax.nn.softmax(lb_params.astype(f32), axis=0), axis=0)
    x2 = x.reshape(T, D)
    for l in range(DEPTH):
        proj = _project(x2, _regroup_w_in(w_in[l]))
        oa = _dsa_t(proj, kv_norm_g[l][None, :], w_uk[l], w_uv[l], rel_bias, B, S)
        ob = _hgrn(proj, lower_bounds[l][None, :], b_norm_g[l][None, :], B, S)
        x1, x1r = _merge(oa, ob, proj, x2, w_br_a[l].astype(bf16), w_br_b[l].astype(bf16), w_o[l].astype(bf16),
                         ln1_g[l][None, :], ln1_b[l][None, :])
        ids, gate = _route(x1, w_pq[l].astype(bf16), sub_keys1[l].astype(bf16), sub_keys2[l].astype(bf16))
        w = _peer_u(ids, x1r, gate, _pack_table(u_table[l]))
        x2 = _peer_v(ids, w, x1r, _pack_table(v_table[l]), ln2_g[l].reshape(8, LANES),
                     ln2_b[l].reshape(8, LANES)).reshape(T, D)
    return x2.reshape(B, S, D)
```

```python
import functools
import math

import numpy as np
import jax
import jax.numpy as jnp
from jax import lax
from jax.experimental import pallas as pl
from jax.experimental.pallas import tpu as pltpu

f32 = jnp.float32
bf16 = jnp.bfloat16
i32 = jnp.int32
HIGHEST = lax.Precision.HIGHEST

D_MODEL = 1024
DEPTH = 1
CHUNK = 64
Q_BLOCK = 128
A_HEADS = 8
A_HEAD_DIM = 64
A_Q = A_HEADS * A_HEAD_DIM
A_LATENT = 128
IDX_HEADS = 8
IDX_DIM = 64
TOPK_MAX = 256
REL_BUCKETS = 32
REL_MAX_DIST = 128
B_HEADS = 4
B_KEY_DIM = 128
B_VAL_DIM = 128
B_K = B_HEADS * B_KEY_DIM
B_V = B_HEADS * B_VAL_DIM
PEER_HEADS = 8
PEER_KEYS = 128
PEER_QDIM = 256
PEER_TOPK = 16
ALPHA = (2 * DEPTH) ** 0.25
LN_EPS = 1e-5
RMS_EPS = 1e-6

IN_SPLITS = (A_Q, A_LATENT, IDX_HEADS * IDX_DIM, IDX_DIM, IDX_HEADS, B_K, B_K, B_V, B_V, D_MODEL, D_MODEL)

COL_QA, COL_QI, COL_QB, COL_FB, COL_IB, COL_GB = 0, 512, 1024, 1536, 2048, 2560
COL_GATE_A, COL_GATE_B, COL_CKV, COL_KIWI = 3072, 4096, 5120, 5248
PROJ_COLS = 5376
LANES = 128
ROWS_PER_EXPERT = 4

INT_MIN = -(2 ** 31)
NEG_BIG = -1e30
VMEM_LIMIT = 48 * 1024 * 1024
TABLE_VMEM_LIMIT = 56 * 1024 * 1024

_NT = (((1,), (1,)), ((), ()))


def _params(*sem, vmem=VMEM_LIMIT):
    return pltpu.CompilerParams(dimension_semantics=sem, vmem_limit_bytes=vmem)


def _proj_body(x_ref, w_ref, o_ref):
    o_ref[...] = jnp.dot(x_ref[...].astype(bf16), w_ref[...], preferred_element_type=f32)


def _project(x2, w, tm=256):
    T = x2.shape[0]
    return pl.pallas_call(
        _proj_body,
        grid=(T // tm,),
        in_specs=[pl.BlockSpec((tm, D_MODEL), lambda i: (i, 0)),
                  pl.BlockSpec((D_MODEL, PROJ_COLS), lambda i: (0, 0))],
        out_specs=pl.BlockSpec((tm, PROJ_COLS), lambda i: (i, 0)),
        out_shape=jax.ShapeDtypeStruct((T, PROJ_COLS), f32),
        compiler_params=_params("arbitrary"),
        name="project",
    )(x2, w)


def _hgrn_body(q_ref, f_ref, i_ref, g_ref, lb_ref, ng_ref, o_ref, st_ref, *, n_chunks):
    @pl.when(pl.program_id(1) == 0)
    def _():
        st_ref[...] = jnp.zeros_like(st_ref)

    lb = lb_ref[...]
    ng = ng_ref[...]
    row = lax.broadcasted_iota(i32, (CHUNK, CHUNK), 0)
    col = lax.broadcasted_iota(i32, (CHUNK, CHUNK), 1)
    causal = row >= col
    tri = causal.astype(f32)
    for n in range(n_chunks):
        sl = pl.ds(n * CHUNK, CHUNK)
        z = f_ref[sl, :]
        log_f = jnp.log(lb + (1.0 - lb) * jax.nn.sigmoid(z))
        kk = (1.0 - lb) * jax.nn.sigmoid(-z)
        bc = jnp.dot(tri, log_f, precision=HIGHEST, preferred_element_type=f32)
        b_mid = bc[CHUNK // 2 - 1:CHUNK // 2, :]
        b_last = bc[CHUNK - 1:CHUNK, :]
        q = q_ref[sl, :]
        v = i_ref[sl, :]
        g = g_ref[sl, :]
        q_in = (q * jnp.exp(bc - b_mid)).astype(bf16)
        k_in = (kk * jnp.exp(b_mid - bc)).astype(bf16)
        k_dec = kk * jnp.exp(b_last - bc)
        q_dec = (q * jnp.exp(bc)).astype(bf16)
        decay = jnp.exp(b_last)
        outs = []
        for h in range(B_HEADS):
            hs = slice(h * B_KEY_DIM, (h + 1) * B_KEY_DIM)
            attn = lax.dot_general(q_in[:, hs], k_in[:, hs], _NT, preferred_element_type=f32)
            attn = jnp.where(causal, attn, 0.0)
            vh = v[:, hs]
            o = jnp.dot(attn.astype(bf16), vh.astype(bf16), preferred_element_type=f32)
            st = st_ref[h]
            o = o + lax.dot_general(q_dec[:, hs], st.astype(bf16), _NT, preferred_element_type=f32)
            d_st = jnp.dot(vh.T.astype(bf16), k_dec[:, hs].astype(bf16), preferred_element_type=f32)
            st_ref[h] = decay[:, hs] * st + d_st
            ms = jnp.mean(o * o, axis=-1, keepdims=True)
            gh = g[:, hs]
            outs.append(o * lax.rsqrt(ms + RMS_EPS) * ng * gh * jax.nn.sigmoid(gh))
        o_ref[sl, :] = jnp.concatenate(outs, axis=1)


def _hgrn(proj, lb, norm_g, B, S, tc=256):
    nc = S // tc
    blk = lambda c: pl.BlockSpec((tc, B_K), lambda b, s, c=c: (b * nc + s, c))
    return pl.pallas_call(
        functools.partial(_hgrn_body, n_chunks=tc // CHUNK),
        grid=(B, nc),
        in_specs=[blk(COL_QB // B_K), blk(COL_FB // B_K), blk(COL_IB // B_K), blk(COL_GB // B_K),
                  pl.BlockSpec((1, B_K), lambda b, s: (0, 0)),
                  pl.BlockSpec((1, B_VAL_DIM), lambda b, s: (0, 0))],
        out_specs=pl.BlockSpec((tc, B_V), lambda b, s: (b * nc + s, 0)),
        out_shape=jax.ShapeDtypeStruct((B * S, B_V), f32),
        scratch_shapes=[pltpu.VMEM((B_HEADS, B_VAL_DIM, B_KEY_DIM), f32)],
        compiler_params=_params("arbitrary", "arbitrary"),
        name="hgrn",
    )(proj, proj, proj, proj, lb, norm_g)


def _t5_bucket(rel):
    half = REL_BUCKETS // 2
    max_exact = half // 2
    base = jnp.where(rel > 0, half, 0)
    n = jnp.abs(rel)
    nf = jnp.maximum(n, 1).astype(f32)
    large = max_exact + (jnp.log(nf / max_exact) / math.log(REL_MAX_DIST / max_exact) * (half - max_exact)).astype(i32)
    large = jnp.minimum(large, half - 1)
    return base + jnp.where(n < max_exact, n, large)


def _paired(n, fn, init=None):
    def pair(i, c):
        return fn(2 * i + 1, fn(2 * i, c))
    c = lax.fori_loop(0, lax.shift_right_logical(n, 1), pair, init)
    return lax.cond((n & 1) == 1, lambda c: fn(n - 1, c), lambda c: c, c)


def _dsa_body(qa_ref, qi_ref, kwq_ref, ckv_ref, kw_ref, g_ref, wuk_ref, wuvt_ref, btt_ref, bfar_ref, o_ref,
                ckvn_ref, ckvnt_ref, skey_ref, s_ref, mt_ref, lt_ref, acc_ref, qlat_ref, qih_ref, wqt_ref,
                mrow_ref, thr_ref, ilim_ref, *, S, n_sel):
    j = pl.program_id(1)
    QB = Q_BLOCK
    n_kb = S // QB
    scale = A_HEAD_DIM ** -0.5
    keyi = lax.broadcasted_iota(i32, (QB, QB), 0)
    qryi = lax.broadcasted_iota(i32, (QB, QB), 1)

    def rows(kb):
        return pl.ds(pl.multiple_of(kb * QB, QB), QB)

    @pl.when(j == 0)
    def _():
        def norm_blk(kb, c):
            cb = ckv_ref[rows(kb), :]
            ms = jnp.mean(cb * cb, axis=-1, keepdims=True)
            cn = cb * lax.rsqrt(ms + RMS_EPS) * g_ref[...]
            ckvn_ref[rows(kb), :] = cn.astype(bf16)
            ckvnt_ref[kb] = cn.T.astype(bf16)
            return c
        lax.fori_loop(0, n_kb, norm_blk, 0)

    qa = qa_ref[...]
    qi = qi_ref[...]
    wqt_ref[...] = kwq_ref[...].T[IDX_DIM:IDX_DIM + IDX_HEADS, :] * ((IDX_HEADS * IDX_DIM) ** -0.5)
    for h in range(A_HEADS):
        qh = qa[:, h * A_HEAD_DIM:(h + 1) * A_HEAD_DIM].astype(bf16)
        qlat_ref[h * QB:(h + 1) * QB, :] = jnp.dot(qh, wuk_ref[h], preferred_element_type=f32).astype(bf16)
    for h in range(IDX_HEADS):
        qih_ref[h * QB:(h + 1) * QB, :] = qi[:, h * IDX_DIM:(h + 1) * IDX_DIM].astype(bf16)

    def index_block(kb, is_current):
        kblk = kw_ref[rows(kb), :][:, :IDX_DIM].astype(bf16)
        lg_all = lax.dot_general(kblk, qih_ref[...], _NT, preferred_element_type=f32)
        sc = jnp.zeros((QB, QB), f32)
        for h in range(IDX_HEADS):
            sc = sc + wqt_ref[h:h + 1, :] * jnp.maximum(lg_all[:, h * QB:(h + 1) * QB], 0.0)
        bits = lax.bitcast_convert_type(sc, i32)
        key = bits ^ ((bits >> 31) & jnp.int32(0x7FFFFFFF))
        if is_current:
            key = jnp.where((qryi < CHUNK) & (keyi >= CHUNK), jnp.int32(INT_MIN), key)
        skey_ref[kb] = key

    def idx_loop(kb, c):
        index_block(kb, False)
        return c
    _paired(j, idx_loop)
    index_block(j, True)

    def count_where(pred_fn):
        def body(kb, acc):
            return acc + jnp.where(pred_fn(kb, skey_ref[kb]), 1.0, 0.0)
        acc = _paired(j + 1, body, jnp.zeros((QB, QB), f32))
        return jnp.sum(acc, axis=0, keepdims=True)

    thr_ref[...] = jnp.full((1, QB), INT_MIN, i32)
    ilim_ref[...] = jnp.full((1, QB), S, i32)
    kf = float(n_sel)

    @pl.when((j + 1) * QB > n_sel)
    def _():
        zero = jnp.zeros((1, QB), i32)
        c0 = count_where(lambda kb, key: key >= 0)
        n_all = jnp.zeros((1, QB), f32) + ((j + 1) * QB).astype(f32)
        thr = jnp.where(c0 >= kf, zero, jnp.int32(INT_MIN))
        c_thr = jnp.where(c0 >= kf, c0, n_all)

        def bit_step(i, carry):
            thr, c_thr = carry
            cand = thr | jnp.left_shift(jnp.int32(1), 30 - i)
            c = count_where(lambda kb, key: key >= cand)
            ok = c >= kf
            return jnp.where(ok, cand, thr), jnp.where(ok, c, c_thr)
        thr, c_thr = lax.fori_loop(0, 31, bit_step, (thr, c_thr))
        thr_ref[...] = thr

        @pl.when(jnp.max(c_thr) > kf)
        def _():
            need = kf - count_where(lambda kb, key: key > thr)
            n_bits = int(math.ceil(math.log2(S + 1)))

            def lim_step(i, lim):
                cand = lim + jnp.left_shift(jnp.int32(1), n_bits - 1 - i)
                c = count_where(lambda kb, key: (key == thr) & (kb * QB + keyi < cand))
                return jnp.where(c <= need, cand, lim)
            ilim_ref[...] = lax.fori_loop(0, n_bits, lim_step, zero)

    mt_ref[...] = jnp.full(mt_ref.shape, NEG_BIG, f32)
    thr = thr_ref[...]
    lim = ilim_ref[...]

    def score_block(kb, bias_of_head):
        key = skey_ref[kb]
        sel = ((key > thr) | ((key == thr) & (kb * QB + keyi < lim))) & (key > jnp.int32(INT_MIN))
        s_all = lax.dot_general(ckvn_ref[rows(kb), :], qlat_ref[...], _NT, preferred_element_type=f32)
        for h in range(A_HEADS):
            s = s_all[:, h * QB:(h + 1) * QB] * scale + bias_of_head(h)
            s = jnp.where(sel, s, NEG_BIG)
            s_ref[kb, h] = s
            mt_ref[h] = jnp.maximum(mt_ref[h], s)

    def far_loop(kb, c):
        score_block(kb, lambda h: bfar_ref[h:h + 1, :])
        return c
    _paired(jnp.maximum(j - 1, 0), far_loop)

    @pl.when(j >= 1)
    def _():
        score_block(j - 1, lambda h: btt_ref[h, 0:QB, :])
    score_block(j, lambda h: btt_ref[h, QB:2 * QB, :])

    for h in range(A_HEADS):
        mrow_ref[h:h + 1, :] = jnp.max(mt_ref[h], axis=0, keepdims=True)
    lt_ref[...] = jnp.zeros(lt_ref.shape, f32)
    acc_ref[...] = jnp.zeros(acc_ref.shape, f32)

    def pv_block(kb, c):
        ps = []
        for h in range(A_HEADS):
            p = jnp.exp(s_ref[kb, h] - mrow_ref[h:h + 1, :])
            lt_ref[h] = lt_ref[h] + p
            ps.append(p.astype(bf16))
        acc_ref[...] = acc_ref[...] + jnp.dot(ckvnt_ref[kb], jnp.concatenate(ps, axis=1),
                                              preferred_element_type=f32)
        return c
    _paired(j + 1, pv_block)

    outs = []
    for h in range(A_HEADS):
        o_lat_t = acc_ref[:, h * QB:(h + 1) * QB] / jnp.sum(lt_ref[h], axis=0, keepdims=True)
        outs.append(jnp.dot(wuvt_ref[h], o_lat_t.astype(bf16), preferred_element_type=f32))
    o_ref[...] = jnp.concatenate(outs, axis=0).T


def _dsa(proj, kv_norm_g, w_uk, w_uv, rel_bias, B, S):
    QB = Q_BLOCK
    nq = S // QB
    n_sel = min(TOPK_MAX, S // 4)
    assert REL_MAX_DIST <= QB
    c = jnp.arange(2 * QB, dtype=i32)[:, None]
    r = jnp.arange(QB, dtype=i32)[None, :]
    onehot = jax.nn.one_hot(_t5_bucket(c - QB - r), REL_BUCKETS, dtype=f32)
    bias_tile_t = jnp.einsum('kqb,bh->hkq', onehot, rel_bias.astype(f32), precision=HIGHEST)
    bias_far = jnp.broadcast_to(rel_bias[_t5_bucket(jnp.int32(-QB - 1))][:, None], (A_HEADS, LANES)).astype(f32)
    wuk_t = jnp.transpose(w_uk, (1, 2, 0)).astype(bf16)
    wuv_t = jnp.transpose(w_uv, (1, 2, 0)).astype(bf16)
    qblk = lambda width, cb: pl.BlockSpec((QB, width), lambda b, j, cb=cb: (b * nq + j, cb))
    seq = lambda cb: pl.BlockSpec((S, LANES), lambda b, j, cb=cb: (b, cb))
    full = lambda a: pl.BlockSpec(a.shape, lambda b, j, n=a.ndim: (0,) * n)
    return pl.pallas_call(
        functools.partial(_dsa_body, S=S, n_sel=n_sel),
        grid=(B, nq),
        in_specs=[qblk(A_Q, COL_QA // A_Q), qblk(A_Q, COL_QI // A_Q), qblk(LANES, COL_KIWI // LANES),
                  seq(COL_CKV // LANES), seq(COL_KIWI // LANES),
                  full(kv_norm_g), full(wuk_t), full(wuv_t), full(bias_tile_t), full(bias_far)],
        out_specs=pl.BlockSpec((QB, A_Q), lambda b, j: (b * nq + j, 0)),
        out_shape=jax.ShapeDtypeStruct((B * S, A_Q), f32),
        scratch_shapes=[
            pltpu.VMEM((S, A_LATENT), bf16),
            pltpu.VMEM((S // QB, A_LATENT, QB), bf16),
            pltpu.VMEM((S // QB, QB, QB), i32),
            pltpu.VMEM((S // QB, A_HEADS, QB, QB), f32),
            pltpu.VMEM((A_HEADS, QB, QB), f32),
            pltpu.VMEM((A_HEADS, QB, QB), f32),
            pltpu.VMEM((A_LATENT, A_HEADS * QB), f32),
            pltpu.VMEM((A_HEADS * QB, A_LATENT), bf16),
            pltpu.VMEM((IDX_HEADS * QB, IDX_DIM), bf16),
            pltpu.VMEM((IDX_HEADS, QB), f32),
            pltpu.VMEM((A_HEADS, QB), f32),
            pltpu.VMEM((1, QB), i32),
            pltpu.VMEM((1, QB), i32),
        ],
        compiler_params=_params("arbitrary", "arbitrary"),
        name="dsa",
    )(proj, proj, proj, proj, proj, kv_norm_g, wuk_t, wuv_t, bias_tile_t, bias_far)


def _layer_norm(y, g, b):
    mu = jnp.mean(y, axis=-1, keepdims=True)
    var = jnp.mean(jnp.square(y - mu), axis=-1, keepdims=True)
    return (y - mu) * lax.rsqrt(var + LN_EPS) * g + b


def _merge_body(oa_ref, ob_ref, ga_ref, gb_ref, x_ref, wa_ref, wb_ref, wo_ref, g_ref, b_ref, o_ref, ot_ref):
    ya = jnp.dot(oa_ref[...].astype(bf16), wa_ref[...], preferred_element_type=f32)
    yb = jnp.dot(ob_ref[...].astype(bf16), wb_ref[...], preferred_element_type=f32)
    merged = jax.nn.sigmoid(ga_ref[...]) * ya + jax.nn.sigmoid(gb_ref[...]) * yb
    mix = jnp.dot(merged.astype(bf16), wo_ref[...], preferred_element_type=f32)
    y = _layer_norm(ALPHA * x_ref[...] + mix, g_ref[...], b_ref[...])
    o_ref[...] = y
    for q in range(D_MODEL // LANES):
        ot_ref[:, q, :] = y[:, q * LANES:(q + 1) * LANES]


def _merge(oa, ob, proj, x2, wa, wb, wo, g, b, tm=256):
    T = x2.shape[0]
    full = lambda a: pl.BlockSpec(a.shape, lambda i: (0, 0))
    return pl.pallas_call(
        _merge_body,
        grid=(T // tm,),
        in_specs=[pl.BlockSpec((tm, A_Q), lambda i: (i, 0)), pl.BlockSpec((tm, B_V), lambda i: (i, 0)),
                  pl.BlockSpec((tm, D_MODEL), lambda i: (i, COL_GATE_A // D_MODEL)),
                  pl.BlockSpec((tm, D_MODEL), lambda i: (i, COL_GATE_B // D_MODEL)),
                  pl.BlockSpec((tm, D_MODEL), lambda i: (i, 0)),
                  full(wa), full(wb), full(wo), full(g), full(b)],
        out_specs=[pl.BlockSpec((tm, D_MODEL), lambda i: (i, 0)),
                   pl.BlockSpec((tm, D_MODEL // LANES, LANES), lambda i: (i, 0, 0))],
        out_shape=[jax.ShapeDtypeStruct((T, D_MODEL), f32),
                   jax.ShapeDtypeStruct((T, D_MODEL // LANES, LANES), f32)],
        compiler_params=_params("arbitrary"),
        name="merge",
    )(oa, ob, proj, proj, x2, wa, wb, wo, g, b)


def _route_body(x_ref, wpq_ref, sk1_ref, sk2_ref, ids_ref, gate_ref,
                v1_ref, i1_ref, v2_ref, i2_ref, ts_ref, ti_ref, idt_ref, gt_ref):
    tb = x_ref.shape[0]
    K = PEER_TOPK
    q = jnp.dot(x_ref[...].astype(bf16), wpq_ref[...], preferred_element_type=f32)
    kiota = lax.broadcasted_iota(i32, (PEER_KEYS, tb), 0).astype(f32)
    r8 = lax.broadcasted_iota(i32, (8, tb), 0).astype(f32)
    r16 = lax.broadcasted_iota(i32, (16, tb), 0).astype(f32)
    flat = jnp.concatenate([r16] + [K * a + r8 for a in range(1, 8)] + [K * (8 + r8)], axis=0)
    half = PEER_QDIM // 2
    for h in range(PEER_HEADS):
        for part, (sk_ref, v_ref, i_ref) in enumerate(((sk1_ref, v1_ref, i1_ref), (sk2_ref, v2_ref, i2_ref))):
            off = h * PEER_QDIM + part * half
            qh = q[:, off:off + half].astype(bf16)
            s = lax.dot_general(sk_ref[...], qh, _NT, preferred_element_type=f32)
            for r in range(K):
                m = jnp.max(s, axis=0, keepdims=True)
                ix = jnp.min(jnp.where(s == m, kiota, float(PEER_KEYS)), axis=0, keepdims=True)
                v_ref[r:r + 1, :] = m
                i_ref[r:r + 1, :] = ix
                s = jnp.where(kiota == ix, -jnp.inf, s)
        v1, i1, v2, i2 = v1_ref[...], i1_ref[...], v2_ref[...], i2_ref[...]
        cand = [v1[0:1] + v2, ] + [v1[a:a + 1] + v2[0:8] for a in range(1, 8)] + [v1[8:16] + v2[0:1]]
        cid = [i1[0:1] * PEER_KEYS + i2, ] + [i1[a:a + 1] * PEER_KEYS + i2[0:8] for a in range(1, 8)] \
            + [i1[8:16] * PEER_KEYS + i2[0:1]]
        cand = jnp.concatenate(cand, axis=0)
        cid = jnp.concatenate(cid, axis=0)
        for r in range(K):
            m = jnp.max(cand, axis=0, keepdims=True)
            fsel = jnp.min(jnp.where(cand == m, flat, float(K * K)), axis=0, keepdims=True)
            hit = flat == fsel
            ts_ref[r:r + 1, :] = m
            ti_ref[r:r + 1, :] = jnp.sum(jnp.where(hit, cid, 0.0), axis=0, keepdims=True)
            cand = jnp.where(hit, -jnp.inf, cand)
        ts = ts_ref[...]
        e = jnp.exp(ts - jnp.max(ts, axis=0, keepdims=True))
        gt_ref[h * K:(h + 1) * K, :] = e / jnp.sum(e, axis=0, keepdims=True)
        idt_ref[h * K:(h + 1) * K, :] = (ti_ref[...] * ROWS_PER_EXPERT).astype(i32)
    gate_ref[...] = gt_ref[...].T
    ids_ref[...] = lax.bitcast_convert_type(lax.bitcast_convert_type(idt_ref[...], f32).T, i32)


def _route(x1, wpq, sk1, sk2, tb=256):
    T = x1.shape[0]
    HK = PEER_HEADS * PEER_TOPK
    full = lambda a: pl.BlockSpec(a.shape, lambda i: (0, 0))
    return pl.pallas_call(
        _route_body,
        grid=(T // tb,),
        in_specs=[pl.BlockSpec((tb, D_MODEL), lambda i: (i, 0)), full(wpq), full(sk1), full(sk2)],
        out_specs=[pl.BlockSpec((tb, HK), lambda i: (i, 0)), pl.BlockSpec((tb, HK), lambda i: (i, 0))],
        out_shape=[jax.ShapeDtypeStruct((T, HK), i32), jax.ShapeDtypeStruct((T, HK), f32)],
        scratch_shapes=[pltpu.VMEM((PEER_TOPK, tb), f32)] * 6 + [
                        pltpu.VMEM((HK, tb), i32), pltpu.VMEM((HK, tb), f32)],
        compiler_params=_params("arbitrary"),
        name="route",
    )(x1, wpq, sk1, sk2)


SLOTS = PEER_HEADS * PEER_TOPK
G_ROWS = SLOTS * ROWS_PER_EXPERT
G_ROWS16 = 2 * G_ROWS
TOKENS_PER_STEP_U = 4
TOKENS_PER_STEP_V = 8


def _pack_table(t):
    E, D = t.shape
    rb = 512
    R = ROWS_PER_EXPERT

    def body(t_ref, o_ref):
        x = t_ref[...]
        for i in range(R):
            lo = lax.bitcast_convert_type(x[:, (2 * i) * LANES:(2 * i + 1) * LANES].astype(jnp.bfloat16).astype(f32), i32)
            hi = lax.bitcast_convert_type(x[:, (2 * i + 1) * LANES:(2 * i + 2) * LANES].astype(jnp.bfloat16).astype(f32), i32)
            o_ref[pl.ds(i, rb, stride=R), :] = (hi & jnp.int32(-65536)) | lax.shift_right_logical(lo, 16)

    return pl.pallas_call(
        body,
        grid=(E // rb,),
        in_specs=[pl.BlockSpec((rb, D), lambda i: (i, 0))],
        out_specs=pl.BlockSpec((rb * R, LANES), lambda i: (i, 0)),
        out_shape=jax.ShapeDtypeStruct((E * R, LANES), i32),
        compiler_params=_params("arbitrary"),
        name="pack_table",
    )(t)


def _gather_rows(rows_ref, t, tab_ref, g_ref):
    R = ROWS_PER_EXPERT
    for k in range(SLOTS):
        r0 = rows_ref[t, k]
        g_ref[R * k:R * (k + 1), :] = tab_ref[pl.ds(pl.multiple_of(r0, R), R), :]


def _split2(a):
    hi = a.astype(bf16)
    return jnp.concatenate([hi, (a - hi.astype(f32)).astype(bf16)], axis=0)


def _dot_select(a, sel_ref):
    hi = a.astype(bf16)
    lo = (a - hi.astype(f32)).astype(bf16)
    return (jnp.dot(hi, sel_ref[...], preferred_element_type=f32)
            + jnp.dot(lo, sel_ref[...], preferred_element_type=f32))


def _token_pipeline(tb, rows_ref, tab_ref, g_scr, issue, finish):
    per_step = g_scr.shape[0]
    for u in range(per_step):
        _gather_rows(rows_ref, u, tab_ref, g_scr.at[u])

    def step(i, c):
        for u in range(per_step):
            t = i * per_step + u
            z = issue(t, pltpu.bitcast(g_scr[u], jnp.bfloat16))
            _gather_rows(rows_ref, jnp.minimum(t + per_step, tb - 1), tab_ref, g_scr.at[u])
            finish(t, z)
        return c
    lax.fori_loop(0, tb // per_step, step, 0)


def _peer_u_body(rows_ref, x_ref, gate_ref, tab_ref, mask_ref, fold_ref, w_ref, g_scr, r_scr):
    tb = x_ref.shape[0]

    def issue(t, gb):
        return lax.dot_general(_split2(x_ref[t]), gb, _NT, preferred_element_type=f32)

    def finish(t, z):
        r_scr[pl.ds(t, 1), :] = jnp.sum((z[0:8] + z[8:16]) * mask_ref[...], axis=0, keepdims=True)

    _token_pipeline(tb, rows_ref, tab_ref, g_scr, issue, finish)
    h = _dot_select(r_scr[...], fold_ref)
    w_ref[...] = gate_ref[...] * (0.5 * h * (1.0 + lax.erf(h * (2.0 ** -0.5))))


def _chunk_mask():
    return (np.arange(G_ROWS16)[None, :] % 8 == np.arange(8)[:, None]).astype(np.float32)


def _peer_u(rows, x1r, gate, tab, tb=256):
    T = rows.shape[0]
    fold = (np.arange(G_ROWS16)[:, None] // 8 == np.arange(SLOTS)[None, :]).astype(np.float32)
    full = lambda a: pl.BlockSpec(a.shape, lambda i: (0, 0))
    return pl.pallas_call(
        _peer_u_body,
        grid=(T // tb,),
        in_specs=[pl.BlockSpec((tb, SLOTS), lambda i: (i, 0), memory_space=pltpu.SMEM),
                  pl.BlockSpec((tb, 8, LANES), lambda i: (i, 0, 0)),
                  pl.BlockSpec((tb, SLOTS), lambda i: (i, 0)),
                  full(tab), pl.BlockSpec((8, G_ROWS16), lambda i: (0, 0)),
                  pl.BlockSpec((G_ROWS16, SLOTS), lambda i: (0, 0))],
        out_specs=pl.BlockSpec((tb, SLOTS), lambda i: (i, 0)),
        out_shape=jax.ShapeDtypeStruct((T, SLOTS), f32),
        scratch_shapes=[pltpu.VMEM((TOKENS_PER_STEP_U, G_ROWS, LANES), i32), pltpu.VMEM((tb, G_ROWS16), f32)],
        compiler_params=_params("arbitrary", vmem=TABLE_VMEM_LIMIT),
        name="peer_u",
    )(rows, x1r, gate, tab, jnp.asarray(_chunk_mask()), jnp.asarray(fold, dtype=jnp.bfloat16))


def _peer_v_body(rows_ref, w_ref, x_ref, tab_ref, exp_ref, mask_ref, g_ref, b_ref, o_ref, g_scr, wexp_scr, y_scr):
    tb = x_ref.shape[0]
    wexp_scr[...] = _dot_select(w_ref[...], exp_ref)

    def issue(t, gb):
        return jnp.dot(_split2(mask_ref[...] * wexp_scr[pl.ds(t, 1), :]), gb, preferred_element_type=f32)

    def finish(t, z):
        y_scr[t] = z[0:8] + z[8:16]

    _token_pipeline(tb, rows_ref, tab_ref, g_scr, issue, finish)
    y = ALPHA * x_ref[...] + y_scr[...]
    n_el = float(D_MODEL)
    mu = jnp.sum(y, axis=(1, 2), keepdims=True) / n_el
    yc = y - mu
    var = jnp.sum(yc * yc, axis=(1, 2), keepdims=True) / n_el
    y_scr[...] = yc * lax.rsqrt(var + LN_EPS) * g_ref[...] + b_ref[...]
    for q in range(D_MODEL // LANES):
        o_ref[:, q * LANES:(q + 1) * LANES] = y_scr[:, q, :]


def _peer_v(rows, w, x1r, tab, g, b, tb=256):
    T = rows.shape[0]
    expand = (np.arange(SLOTS)[:, None] == np.arange(G_ROWS16)[None, :] // 8).astype(np.float32)
    full = lambda a: pl.BlockSpec(a.shape, lambda i: (0, 0))
    return pl.pallas_call(
        _peer_v_body,
        grid=(T // tb,),
        in_specs=[pl.BlockSpec((tb, SLOTS), lambda i: (i, 0), memory_space=pltpu.SMEM),
                  pl.BlockSpec((tb, SLOTS), lambda i: (i, 0)),
                  pl.BlockSpec((tb, 8, LANES), lambda i: (i, 0, 0)),
                  full(tab), pl.BlockSpec((SLOTS, G_ROWS16), lambda i: (0, 0)),
                  pl.BlockSpec((8, G_ROWS16), lambda i: (0, 0)),
                  pl.BlockSpec((8, LANES), lambda i: (0, 0)),
                  pl.BlockSpec((8, LANES), lambda i: (0, 0))],
        out_specs=pl.BlockSpec((tb, D_MODEL), lambda i: (i, 0)),
        out_shape=jax.ShapeDtypeStruct((T, D_MODEL), f32),
        scratch_shapes=[pltpu.VMEM((TOKENS_PER_STEP_V, G_ROWS, LANES), i32), pltpu.VMEM((tb, G_ROWS16), f32),
                        pltpu.VMEM((tb, 8, LANES), f32)],
        compiler_params=_params("arbitrary", vmem=TABLE_VMEM_LIMIT),
        name="peer_v",
    )(rows, w, x1r, tab, jnp.asarray(expand, dtype=jnp.bfloat16), jnp.asarray(_chunk_mask()), g, b)


def _regroup_w_in(w):
    offs = np.cumsum(np.array(IN_SPLITS))[:-1].tolist()
    qa, ckv, qi, ki, wi, qb, fb, ib, gb, gate_a, gate_b = jnp.split(w, offs, axis=-1)
    pad = jnp.zeros((w.shape[0], LANES - IDX_DIM - IDX_HEADS), w.dtype)
    out = jnp.concatenate([qa, qi, qb, fb, ib, gb, gate_a, gate_b, ckv, ki, wi, pad], axis=-1)
    assert out.shape[-1] == PROJ_COLS
    return out.astype(bf16)


def kernel(x, w_in, kv_norm_g, w_uk, w_uv, rel_bias, lb_params, b_norm_g, w_br_a, w_br_b, w_o, ln1_g, ln1_b,
           w_pq, sub_keys1, sub_keys2, u_table, v_table, ln2_g, ln2_b):
    B, S, D = x.shape
    T = B * S
    lower_bounds = jnp.cumsum(jax.nn.softmax(lb_params.astype(f32), axis=0), axis=0)
    x2 = x.reshape(T, D)
    for l in range(DEPTH):
        proj = _project(x2, _regroup_w_in(w_in[l]))
        oa = _dsa(proj, kv_norm_g[l][None, :], w_uk[l], w_uv[l], rel_bias, B, S)
        ob = _hgrn(proj, lower_bounds[l][None, :], b_norm_g[l][None, :], B, S)
        x1, x1r = _merge(oa, ob, proj, x2, w_br_a[l].astype(bf16), w_br_b[l].astype(bf16), w_o[l].astype(bf16),
                         ln1_g[l][None, :], ln1_b[l][None, :])
        ids, gate = _route(x1, w_pq[l].astype(bf16), sub_keys1[l].astype(bf16), sub_keys2[l].astype(bf16))
        w = _peer_u(ids, x1r, gate, _pack_table(u_table[l]))
        x2 = _peer_v(ids, w, x1r, _pack_table(v_table[l]), ln2_g[l].reshape(8, LANES),
                     ln2_b[l].reshape(8, LANES)).reshape(T, D)
    return x2.reshape(B, S, D)
```

```python
import functools
import math

import numpy as np
import jax
import jax.numpy as jnp
from jax import lax
from jax.experimental import pallas as pl
from jax.experimental.pallas import tpu as pltpu

f32 = jnp.float32
bf16 = jnp.bfloat16
i32 = jnp.int32
HIGHEST = lax.Precision.HIGHEST

D_MODEL = 1024
DEPTH = 1
CHUNK = 64
Q_BLOCK = 128
A_HEADS = 8
A_HEAD_DIM = 64
A_Q = A_HEADS * A_HEAD_DIM
A_LATENT = 128
IDX_HEADS = 8
IDX_DIM = 64
TOPK_MAX = 256
REL_BUCKETS = 32
REL_MAX_DIST = 128
B_HEADS = 4
B_KEY_DIM = 128
B_VAL_DIM = 128
B_K = B_HEADS * B_KEY_DIM
B_V = B_HEADS * B_VAL_DIM
PEER_HEADS = 8
PEER_KEYS = 128
PEER_QDIM = 256
PEER_TOPK = 16
ALPHA = (2 * DEPTH) ** 0.25
LN_EPS = 1e-5
RMS_EPS = 1e-6

IN_SPLITS = (A_Q, A_LATENT, IDX_HEADS * IDX_DIM, IDX_DIM, IDX_HEADS, B_K, B_K, B_V, B_V, D_MODEL, D_MODEL)

COL_QA, COL_QI, COL_QB, COL_FB, COL_IB, COL_GB = 0, 512, 1024, 1536, 2048, 2560
COL_GATE_A, COL_GATE_B, COL_CKV, COL_KIWI = 3072, 4096, 5120, 5248
PROJ_COLS = 5376
LANES = 128
ROWS_PER_EXPERT = 4

INT_MIN = -(2 ** 31)
NEG_BIG = -1e30
VMEM_LIMIT = 48 * 1024 * 1024
TABLE_VMEM_LIMIT = 56 * 1024 * 1024

_NT = (((1,), (1,)), ((), ()))


def _params(*sem, vmem=VMEM_LIMIT):
    return pltpu.CompilerParams(dimension_semantics=sem, vmem_limit_bytes=vmem)


def _proj_body(x_ref, w_ref, o_ref):
    o_ref[...] = jnp.dot(x_ref[...].astype(bf16), w_ref[...], preferred_element_type=f32)


def _project(x2, w, tm=256):
    T = x2.shape[0]
    return pl.pallas_call(
        _proj_body,
        grid=(T // tm,),
        in_specs=[pl.BlockSpec((tm, D_MODEL), lambda i: (i, 0)),
                  pl.BlockSpec((D_MODEL, PROJ_COLS), lambda i: (0, 0))],
        out_specs=pl.BlockSpec((tm, PROJ_COLS), lambda i: (i, 0)),
        out_shape=jax.ShapeDtypeStruct((T, PROJ_COLS), f32),
        compiler_params=_params("arbitrary"),
        name="project",
    )(x2, w)


def _hgrn_body(q_ref, f_ref, i_ref, g_ref, lb_ref, ng_ref, o_ref, st_ref, *, n_chunks):
    @pl.when(pl.program_id(1) == 0)
    def _():
        st_ref[...] = jnp.zeros_like(st_ref)

    lb = lb_ref[...]
    ng = ng_ref[...]
    row = lax.broadcasted_iota(i32, (CHUNK, CHUNK), 0)
    col = lax.broadcasted_iota(i32, (CHUNK, CHUNK), 1)
    causal = row >= col
    tri = causal.astype(f32)
    for n in range(n_chunks):
        sl = pl.ds(n * CHUNK, CHUNK)
        z = f_ref[sl, :]
        log_f = jnp.log(lb + (1.0 - lb) * jax.nn.sigmoid(z))
        kk = (1.0 - lb) * jax.nn.sigmoid(-z)
        bc = jnp.dot(tri, log_f, precision=HIGHEST, preferred_element_type=f32)
        b_mid = bc[CHUNK // 2 - 1:CHUNK // 2, :]
        b_last = bc[CHUNK - 1:CHUNK, :]
        q = q_ref[sl, :]
        v = i_ref[sl, :]
        g = g_ref[sl, :]
        q_in = (q * jnp.exp(bc - b_mid)).astype(bf16)
        k_in = (kk * jnp.exp(b_mid - bc)).astype(bf16)
        k_dec = kk * jnp.exp(b_last - bc)
        q_dec = (q * jnp.exp(bc)).astype(bf16)
        decay = jnp.exp(b_last)
        outs = []
        for h in range(B_HEADS):
            hs = slice(h * B_KEY_DIM, (h + 1) * B_KEY_DIM)
            attn = lax.dot_general(q_in[:, hs], k_in[:, hs], _NT, preferred_element_type=f32)
            attn = jnp.where(causal, attn, 0.0)
            vh = v[:, hs]
            o = jnp.dot(attn.astype(bf16), vh.astype(bf16), preferred_element_type=f32)
            st = st_ref[h]
            o = o + lax.dot_general(q_dec[:, hs], st.astype(bf16), _NT, preferred_element_type=f32)
            d_st = jnp.dot(vh.T.astype(bf16), k_dec[:, hs].astype(bf16), preferred_element_type=f32)
            st_ref[h] = decay[:, hs] * st + d_st
            ms = jnp.mean(o * o, axis=-1, keepdims=True)
            gh = g[:, hs]
            outs.append(o * lax.rsqrt(ms + RMS_EPS) * ng * gh * jax.nn.sigmoid(gh))
        o_ref[sl, :] = jnp.concatenate(outs, axis=1)


def _hgrn(proj, lb, norm_g, B, S, tc=256):
    nc = S // tc
    blk = lambda c: pl.BlockSpec((tc, B_K), lambda b, s, c=c: (b * nc + s, c))
    return pl.pallas_call(
        functools.partial(_hgrn_body, n_chunks=tc // CHUNK),
        grid=(B, nc),
        in_specs=[blk(COL_QB // B_K), blk(COL_FB // B_K), blk(COL_IB // B_K), blk(COL_GB // B_K),
                  pl.BlockSpec((1, B_K), lambda b, s: (0, 0)),
                  pl.BlockSpec((1, B_VAL_DIM), lambda b, s: (0, 0))],
        out_specs=pl.BlockSpec((tc, B_V), lambda b, s: (b * nc + s, 0)),
        out_shape=jax.ShapeDtypeStruct((B * S, B_V), f32),
        scratch_shapes=[pltpu.VMEM((B_HEADS, B_VAL_DIM, B_KEY_DIM), f32)],
        compiler_params=_params("arbitrary", "arbitrary"),
        name="hgrn",
    )(proj, proj, proj, proj, lb, norm_g)


def _t5_bucket(rel):
    half = REL_BUCKETS // 2
    max_exact = half // 2
    base = jnp.where(rel > 0, half, 0)
    n = jnp.abs(rel)
    nf = jnp.maximum(n, 1).astype(f32)
    large = max_exact + (jnp.log(nf / max_exact) / math.log(REL_MAX_DIST / max_exact) * (half - max_exact)).astype(i32)
    large = jnp.minimum(large, half - 1)
    return base + jnp.where(n < max_exact, n, large)


def _paired(n, fn, init=None):
    def pair(i, c):
        return fn(2 * i + 1, fn(2 * i, c))
    c = lax.fori_loop(0, lax.shift_right_logical(n, 1), pair, init)
    return lax.cond((n & 1) == 1, lambda c: fn(n - 1, c), lambda c: c, c)


def _dsa_body(qa_ref, qi_ref, kwq_ref, ckv_ref, kw_ref, g_ref, wuk_ref, wuvt_ref, btt_ref, bfar_ref, o_ref,
                ckvn_ref, ckvnt_ref, skey_ref, s_ref, mt_ref, lt_ref, acc_ref, qlat_ref, qih_ref, wqt_ref,
                mrow_ref, thr_ref, ilim_ref, *, S, n_sel):
    j = pl.program_id(1)
    QB = Q_BLOCK
    n_kb = S // QB
    scale = A_HEAD_DIM ** -0.5
    keyi = lax.broadcasted_iota(i32, (QB, QB), 0)
    qryi = lax.broadcasted_iota(i32, (QB, QB), 1)

    def rows(kb):
        return pl.ds(pl.multiple_of(kb * QB, QB), QB)

    @pl.when(j == 0)
    def _():
        def norm_blk(kb, c):
            cb = ckv_ref[rows(kb), :]
            ms = jnp.mean(cb * cb, axis=-1, keepdims=True)
            cn = cb * lax.rsqrt(ms + RMS_EPS) * g_ref[...]
            ckvn_ref[rows(kb), :] = cn.astype(bf16)
            ckvnt_ref[kb] = cn.T.astype(bf16)
            return c
        lax.fori_loop(0, n_kb, norm_blk, 0)

    qa = qa_ref[...]
    qi = qi_ref[...]
    wqt_ref[...] = kwq_ref[...].T[IDX_DIM:IDX_DIM + IDX_HEADS, :] * ((IDX_HEADS * IDX_DIM) ** -0.5)
    for h in range(A_HEADS):
        qh = qa[:, h * A_HEAD_DIM:(h + 1) * A_HEAD_DIM].astype(bf16)
        qlat_ref[h * QB:(h + 1) * QB, :] = jnp.dot(qh, wuk_ref[h], preferred_element_type=f32).astype(bf16)
    for h in range(IDX_HEADS):
        qih_ref[h * QB:(h + 1) * QB, :] = qi[:, h * IDX_DIM:(h + 1) * IDX_DIM].astype(bf16)

    def index_block(kb, is_current):
        kblk = kw_ref[rows(kb), :][:, :IDX_DIM].astype(bf16)
        lg_all = lax.dot_general(kblk, qih_ref[...], _NT, preferred_element_type=f32)
        sc = jnp.zeros((QB, QB), f32)
        for h in range(IDX_HEADS):
            sc = sc + wqt_ref[h:h + 1, :] * jnp.maximum(lg_all[:, h * QB:(h + 1) * QB], 0.0)
        bits = lax.bitcast_convert_type(sc, i32)
        key = bits ^ ((bits >> 31) & jnp.int32(0x7FFFFFFF))
        if is_current:
            key = jnp.where((qryi < CHUNK) & (keyi >= CHUNK), jnp.int32(INT_MIN), key)
        skey_ref[kb] = key

    def idx_loop(kb, c):
        index_block(kb, False)
        return c
    _paired(j, idx_loop)
    index_block(j, True)

    def count_where(pred_fn):
        def body(kb, acc):
            return acc + jnp.where(pred_fn(kb, skey_ref[kb]), 1.0, 0.0)
        acc = _paired(j + 1, body, jnp.zeros((QB, QB), f32))
        return jnp.sum(acc, axis=0, keepdims=True)

    thr_ref[...] = jnp.full((1, QB), INT_MIN, i32)
    ilim_ref[...] = jnp.full((1, QB), S, i32)
    kf = float(n_sel)

    @pl.when((j + 1) * QB > n_sel)
    def _():
        zero = jnp.zeros((1, QB), i32)
        c0 = count_where(lambda kb, key: key >= 0)
        n_all = jnp.zeros((1, QB), f32) + ((j + 1) * QB).astype(f32)
        thr = jnp.where(c0 >= kf, zero, jnp.int32(INT_MIN))
        c_thr = jnp.where(c0 >= kf, c0, n_all)

        def bit_step(i, carry):
            thr, c_thr = carry
            cand = thr | jnp.left_shift(jnp.int32(1), 30 - i)
            c = count_where(lambda kb, key: key >= cand)
            ok = c >= kf
            return jnp.where(ok, cand, thr), jnp.where(ok, c, c_thr)
        thr, c_thr = lax.fori_loop(0, 31, bit_step, (thr, c_thr))
        thr_ref[...] = thr

        @pl.when(jnp.max(c_thr) > kf)
        def _():
            need = kf - count_where(lambda kb, key: key > thr)
            n_bits = int(math.ceil(math.log2(S + 1)))

            def lim_step(i, lim):
                cand = lim + jnp.left_shift(jnp.int32(1), n_bits - 1 - i)
                c = count_where(lambda kb, key: (key == thr) & (kb * QB + keyi < cand))
                return jnp.where(c <= need, cand, lim)
            ilim_ref[...] = lax.fori_loop(0, n_bits, lim_step, zero)

    mt_ref[...] = jnp.full(mt_ref.shape, NEG_BIG, f32)
    thr = thr_ref[...]
    lim = ilim_ref[...]

    def score_block(kb, bias_of_head):
        key = skey_ref[kb]
        sel = ((key > thr) | ((key == thr) & (kb * QB + keyi < lim))) & (key > jnp.int32(INT_MIN))
        s_all = lax.dot_general(ckvn_ref[rows(kb), :], qlat_ref[...], _NT, preferred_element_type=f32)
        for h in range(A_HEADS):
            s = s_all[:, h * QB:(h + 1) * QB] * scale + bias_of_head(h)
            s = jnp.where(sel, s, NEG_BIG)
            s_ref[kb, h] = s
            mt_ref[h] = jnp.maximum(mt_ref[h], s)

    def far_loop(kb, c):
        score_block(kb, lambda h: bfar_ref[h:h + 1, :])
        return c
    _paired(jnp.maximum(j - 1, 0), far_loop)

    @pl.when(j >= 1)
    def _():
        score_block(j - 1, lambda h: btt_ref[h, 0:QB, :])
    score_block(j, lambda h: btt_ref[h, QB:2 * QB, :])

    for h in range(A_HEADS):
        mrow_ref[h:h + 1, :] = jnp.max(mt_ref[h], axis=0, keepdims=True)
    lt_ref[...] = jnp.zeros(lt_ref.shape, f32)
    acc_ref[...] = jnp.zeros(acc_ref.shape, f32)

    def pv_block(kb, c):
        ps = []
        for h in range(A_HEADS):
            p = jnp.exp(s_ref[kb, h] - mrow_ref[h:h + 1, :])
            lt_ref[h] = lt_ref[h] + p
            ps.append(p.astype(bf16))
        acc_ref[...] = acc_ref[...] + jnp.dot(ckvnt_ref[kb], jnp.concatenate(ps, axis=1),
                                              preferred_element_type=f32)
        return c
    _paired(j + 1, pv_block)

    outs = []
    for h in range(A_HEADS):
        o_lat_t = acc_ref[:, h * QB:(h + 1) * QB] / jnp.sum(lt_ref[h], axis=0, keepdims=True)
        outs.append(jnp.dot(wuvt_ref[h], o_lat_t.astype(bf16), preferred_element_type=f32))
    o_ref[...] = jnp.concatenate(outs, axis=0).T


def _dsa(proj, kv_norm_g, w_uk, w_uv, rel_bias, B, S):
    QB = Q_BLOCK
    nq = S // QB
    n_sel = min(TOPK_MAX, S // 4)
    assert REL_MAX_DIST <= QB
    c = jnp.arange(2 * QB, dtype=i32)[:, None]
    r = jnp.arange(QB, dtype=i32)[None, :]
    onehot = jax.nn.one_hot(_t5_bucket(c - QB - r), REL_BUCKETS, dtype=f32)
    bias_tile_t = jnp.einsum('kqb,bh->hkq', onehot, rel_bias.astype(f32), precision=HIGHEST)
    bias_far = jnp.broadcast_to(rel_bias[_t5_bucket(jnp.int32(-QB - 1))][:, None], (A_HEADS, LANES)).astype(f32)
    wuk_t = jnp.transpose(w_uk, (1, 2, 0)).astype(bf16)
    wuv_t = jnp.transpose(w_uv, (1, 2, 0)).astype(bf16)
    qblk = lambda width, cb: pl.BlockSpec((QB, width), lambda b, j, cb=cb: (b * nq + j, cb))
    seq = lambda cb: pl.BlockSpec((S, LANES), lambda b, j, cb=cb: (b, cb))
    full = lambda a: pl.BlockSpec(a.shape, lambda b, j, n=a.ndim: (0,) * n)
    return pl.pallas_call(
        functools.partial(_dsa_body, S=S, n_sel=n_sel),
        grid=(B, nq),
        in_specs=[qblk(A_Q, COL_QA // A_Q), qblk(A_Q, COL_QI // A_Q), qblk(LANES, COL_KIWI // LANES),
                  seq(COL_CKV // LANES), seq(COL_KIWI // LANES),
                  full(kv_norm_g), full(wuk_t), full(wuv_t), full(bias_tile_t), full(bias_far)],
        out_specs=pl.BlockSpec((QB, A_Q), lambda b, j: (b * nq + j, 0)),
        out_shape=jax.ShapeDtypeStruct((B * S, A_Q), f32),
        scratch_shapes=[
            pltpu.VMEM((S, A_LATENT), bf16),
            pltpu.VMEM((S // QB, A_LATENT, QB), bf16),
            pltpu.VMEM((S // QB, QB, QB), i32),
            pltpu.VMEM((S // QB, A_HEADS, QB, QB), f32),
            pltpu.VMEM((A_HEADS, QB, QB), f32),
            pltpu.VMEM((A_HEADS, QB, QB), f32),
            pltpu.VMEM((A_LATENT, A_HEADS * QB), f32),
            pltpu.VMEM((A_HEADS * QB, A_LATENT), bf16),
            pltpu.VMEM((IDX_HEADS * QB, IDX_DIM), bf16),
            pltpu.VMEM((IDX_HEADS, QB), f32),
            pltpu.VMEM((A_HEADS, QB), f32),
            pltpu.VMEM((1, QB), i32),
            pltpu.VMEM((1, QB), i32),
        ],
        compiler_params=_params("arbitrary", "arbitrary"),
        name="dsa",
    )(proj, proj, proj, proj, proj, kv_norm_g, wuk_t, wuv_t, bias_tile_t, bias_far)


def _layer_norm(y, g, b):
    mu = jnp.mean(y, axis=-1, keepdims=True)
    var = jnp.mean(jnp.square(y - mu), axis=-1, keepdims=True)
    return (y - mu) * lax.rsqrt(var + LN_EPS) * g + b


def _route_from(xb, wpq_ref, sk1_ref, sk2_ref, ids_ref, gate_ref,
                v1_ref, i1_ref, v2_ref, i2_ref, ts_ref, ti_ref, idt_ref, gt_ref):
    tb = xb.shape[0]
    K = PEER_TOPK
    q = jnp.dot(xb, wpq_ref[...], preferred_element_type=f32)
    kiota = lax.broadcasted_iota(i32, (PEER_KEYS, tb), 0).astype(f32)
    r8 = lax.broadcasted_iota(i32, (8, tb), 0).astype(f32)
    r16 = lax.broadcasted_iota(i32, (16, tb), 0).astype(f32)
    flat = jnp.concatenate([r16] + [K * a + r8 for a in range(1, 8)] + [K * (8 + r8)], axis=0)
    half = PEER_QDIM // 2
    for h in range(PEER_HEADS):
        for part, (sk_ref, v_ref, i_ref) in enumerate(((sk1_ref, v1_ref, i1_ref), (sk2_ref, v2_ref, i2_ref))):
            off = h * PEER_QDIM + part * half
            qh = q[:, off:off + half].astype(bf16)
            s = lax.dot_general(sk_ref[...], qh, _NT, preferred_element_type=f32)
            for r in range(K):
                m = jnp.max(s, axis=0, keepdims=True)
                ix = jnp.min(jnp.where(s == m, kiota, float(PEER_KEYS)), axis=0, keepdims=True)
                v_ref[r:r + 1, :] = m
                i_ref[r:r + 1, :] = ix
                s = jnp.where(kiota == ix, -jnp.inf, s)
        v1, i1, v2, i2 = v1_ref[...], i1_ref[...], v2_ref[...], i2_ref[...]
        cand = [v1[0:1] + v2, ] + [v1[a:a + 1] + v2[0:8] for a in range(1, 8)] + [v1[8:16] + v2[0:1]]
        cid = [i1[0:1] * PEER_KEYS + i2, ] + [i1[a:a + 1] * PEER_KEYS + i2[0:8] for a in range(1, 8)] \
            + [i1[8:16] * PEER_KEYS + i2[0:1]]
        cand = jnp.concatenate(cand, axis=0)
        cid = jnp.concatenate(cid, axis=0)
        for r in range(K):
            m = jnp.max(cand, axis=0, keepdims=True)
            fsel = jnp.min(jnp.where(cand == m, flat, float(K * K)), axis=0, keepdims=True)
            hit = flat == fsel
            ts_ref[r:r + 1, :] = m
            ti_ref[r:r + 1, :] = jnp.sum(jnp.where(hit, cid, 0.0), axis=0, keepdims=True)
            cand = jnp.where(hit, -jnp.inf, cand)
        ts = ts_ref[...]
        e = jnp.exp(ts - jnp.max(ts, axis=0, keepdims=True))
        gt_ref[h * K:(h + 1) * K, :] = e / jnp.sum(e, axis=0, keepdims=True)
        idt_ref[h * K:(h + 1) * K, :] = (ti_ref[...] * ROWS_PER_EXPERT).astype(i32)
    gate_ref[...] = gt_ref[...].T
    ids_ref[...] = lax.bitcast_convert_type(lax.bitcast_convert_type(idt_ref[...], f32).T, i32)


def _merge_route_body(oa_ref, ob_ref, ga_ref, gb_ref, x_ref, wa_ref, wb_ref, wo_ref, g_ref, b_ref,
                      wpq_ref, sk1_ref, sk2_ref, ot_ref, ids_ref, gate_ref, y_scr, *scratch):
    @pl.when(pl.program_id(0) == 0)
    def _():
        y_scr[...] = jnp.zeros_like(y_scr)

    _route_from(y_scr[...], wpq_ref, sk1_ref, sk2_ref, ids_ref, gate_ref, *scratch)

    ya = jnp.dot(oa_ref[...].astype(bf16), wa_ref[...], preferred_element_type=f32)
    yb = jnp.dot(ob_ref[...].astype(bf16), wb_ref[...], preferred_element_type=f32)
    merged = jax.nn.sigmoid(ga_ref[...]) * ya + jax.nn.sigmoid(gb_ref[...]) * yb
    mix = jnp.dot(merged.astype(bf16), wo_ref[...], preferred_element_type=f32)
    y = _layer_norm(ALPHA * x_ref[...] + mix, g_ref[...], b_ref[...])
    for q in range(D_MODEL // LANES):
        ot_ref[:, q, :] = y[:, q * LANES:(q + 1) * LANES]
    y_scr[...] = y.astype(bf16)


def _merge_route(oa, ob, proj, x2, wa, wb, wo, g, b, wpq, sk1, sk2, tb=256):
    T = x2.shape[0]
    n = T // tb
    HK = PEER_HEADS * PEER_TOPK
    full = lambda a: pl.BlockSpec(a.shape, lambda i: (0, 0))
    cur = lambda i: jnp.minimum(i, n - 1)
    prev = lambda i: jnp.maximum(i - 1, 0)
    return pl.pallas_call(
        _merge_route_body,
        grid=(n + 1,),
        in_specs=[pl.BlockSpec((tb, A_Q), lambda i: (cur(i), 0)), pl.BlockSpec((tb, B_V), lambda i: (cur(i), 0)),
                  pl.BlockSpec((tb, D_MODEL), lambda i: (cur(i), COL_GATE_A // D_MODEL)),
                  pl.BlockSpec((tb, D_MODEL), lambda i: (cur(i), COL_GATE_B // D_MODEL)),
                  pl.BlockSpec((tb, D_MODEL), lambda i: (cur(i), 0)),
                  full(wa), full(wb), full(wo), full(g), full(b), full(wpq), full(sk1), full(sk2)],
        out_specs=[pl.BlockSpec((tb, D_MODEL // LANES, LANES), lambda i: (cur(i), 0, 0)),
                   pl.BlockSpec((tb, HK), lambda i: (prev(i), 0)), pl.BlockSpec((tb, HK), lambda i: (prev(i), 0))],
        out_shape=[jax.ShapeDtypeStruct((T, D_MODEL // LANES, LANES), f32),
                   jax.ShapeDtypeStruct((T, HK), i32), jax.ShapeDtypeStruct((T, HK), f32)],
        scratch_shapes=[pltpu.VMEM((tb, D_MODEL), bf16)] + [pltpu.VMEM((PEER_TOPK, tb), f32)] * 6 + [
                        pltpu.VMEM((HK, tb), i32), pltpu.VMEM((HK, tb), f32)],
        compiler_params=_params("arbitrary"),
        name="merge_route",
    )(oa, ob, proj, proj, x2, wa, wb, wo, g, b, wpq, sk1, sk2)


SLOTS = PEER_HEADS * PEER_TOPK
G_ROWS = SLOTS * ROWS_PER_EXPERT
G_ROWS16 = 2 * G_ROWS
TOKENS_PER_STEP_U = 4
TOKENS_PER_STEP_V = 8


def _pack_table(t):
    E, D = t.shape
    rb = 512
    R = ROWS_PER_EXPERT

    def body(t_ref, o_ref):
        x = t_ref[...]
        for i in range(R):
            lo = lax.bitcast_convert_type(x[:, (2 * i) * LANES:(2 * i + 1) * LANES].astype(jnp.bfloat16).astype(f32), i32)
            hi = lax.bitcast_convert_type(x[:, (2 * i + 1) * LANES:(2 * i + 2) * LANES].astype(jnp.bfloat16).astype(f32), i32)
            o_ref[pl.ds(i, rb, stride=R), :] = (hi & jnp.int32(-65536)) | lax.shift_right_logical(lo, 16)

    return pl.pallas_call(
        body,
        grid=(E // rb,),
        in_specs=[pl.BlockSpec((rb, D), lambda i: (i, 0))],
        out_specs=pl.BlockSpec((rb * R, LANES), lambda i: (i, 0)),
        out_shape=jax.ShapeDtypeStruct((E * R, LANES), i32),
        compiler_params=_params("arbitrary"),
        name="pack_table",
    )(t)


def _gather_rows(rows_ref, t, tab_ref, g_ref):
    R = ROWS_PER_EXPERT
    for k in range(SLOTS):
        r0 = rows_ref[t, k]
        g_ref[R * k:R * (k + 1), :] = tab_ref[pl.ds(pl.multiple_of(r0, R), R), :]


def _split2(a):
    hi = a.astype(bf16)
    return jnp.concatenate([hi, (a - hi.astype(f32)).astype(bf16)], axis=0)


def _dot_select(a, sel_ref):
    hi = a.astype(bf16)
    lo = (a - hi.astype(f32)).astype(bf16)
    return (jnp.dot(hi, sel_ref[...], preferred_element_type=f32)
            + jnp.dot(lo, sel_ref[...], preferred_element_type=f32))


def _token_pipeline(tb, rows_ref, tab_ref, g_scr, issue, finish):
    per_step = g_scr.shape[0]
    for u in range(per_step):
        _gather_rows(rows_ref, u, tab_ref, g_scr.at[u])

    def step(i, c):
        for u in range(per_step):
            t = i * per_step + u
            z = issue(t, pltpu.bitcast(g_scr[u], jnp.bfloat16))
            _gather_rows(rows_ref, jnp.minimum(t + per_step, tb - 1), tab_ref, g_scr.at[u])
            finish(t, z)
        return c
    lax.fori_loop(0, tb // per_step, step, 0)


def _peer_u_body(rows_ref, x_ref, gate_ref, tab_ref, mask_ref, fold_ref, w_ref, g_scr, r_scr):
    tb = x_ref.shape[0]

    def issue(t, gb):
        return lax.dot_general(_split2(x_ref[t]), gb, _NT, preferred_element_type=f32)

    def finish(t, z):
        r_scr[pl.ds(t, 1), :] = jnp.sum((z[0:8] + z[8:16]) * mask_ref[...], axis=0, keepdims=True)

    _token_pipeline(tb, rows_ref, tab_ref, g_scr, issue, finish)
    h = _dot_select(r_scr[...], fold_ref)
    w_ref[...] = gate_ref[...] * (0.5 * h * (1.0 + lax.erf(h * (2.0 ** -0.5))))


def _chunk_mask():
    return (np.arange(G_ROWS16)[None, :] % 8 == np.arange(8)[:, None]).astype(np.float32)


def _peer_u(rows, x1r, gate, tab, tb=256):
    T = rows.shape[0]
    fold = (np.arange(G_ROWS16)[:, None] // 8 == np.arange(SLOTS)[None, :]).astype(np.float32)
    full = lambda a: pl.BlockSpec(a.shape, lambda i: (0, 0))
    return pl.pallas_call(
        _peer_u_body,
        grid=(T // tb,),
        in_specs=[pl.BlockSpec((tb, SLOTS), lambda i: (i, 0), memory_space=pltpu.SMEM),
                  pl.BlockSpec((tb, 8, LANES), lambda i: (i, 0, 0)),
                  pl.BlockSpec((tb, SLOTS), lambda i: (i, 0)),
                  full(tab), pl.BlockSpec((8, G_ROWS16), lambda i: (0, 0)),
                  pl.BlockSpec((G_ROWS16, SLOTS), lambda i: (0, 0))],
        out_specs=pl.BlockSpec((tb, SLOTS), lambda i: (i, 0)),
        out_shape=jax.ShapeDtypeStruct((T, SLOTS), f32),
        scratch_shapes=[pltpu.VMEM((TOKENS_PER_STEP_U, G_ROWS, LANES), i32), pltpu.VMEM((tb, G_ROWS16), f32)],
        compiler_params=_params("arbitrary", vmem=TABLE_VMEM_LIMIT),
        name="peer_u",
    )(rows, x1r, gate, tab, jnp.asarray(_chunk_mask()), jnp.asarray(fold, dtype=jnp.bfloat16))


def _peer_v_body(rows_ref, w_ref, x_ref, tab_ref, exp_ref, mask_ref, g_ref, b_ref, o_ref, g_scr, wexp_scr, y_scr):
    tb = x_ref.shape[0]
    wexp_scr[...] = _dot_select(w_ref[...], exp_ref)

    def issue(t, gb):
        return jnp.dot(_split2(mask_ref[...] * wexp_scr[pl.ds(t, 1), :]), gb, preferred_element_type=f32)

    def finish(t, z):
        y_scr[t] = z[0:8] + z[8:16]

    _token_pipeline(tb, rows_ref, tab_ref, g_scr, issue, finish)
    y = ALPHA * x_ref[...] + y_scr[...]
    n_el = float(D_MODEL)
    mu = jnp.sum(y, axis=(1, 2), keepdims=True) / n_el
    yc = y - mu
    var = jnp.sum(yc * yc, axis=(1, 2), keepdims=True) / n_el
    y_scr[...] = yc * lax.rsqrt(var + LN_EPS) * g_ref[...] + b_ref[...]
    for q in range(D_MODEL // LANES):
        o_ref[:, q * LANES:(q + 1) * LANES] = y_scr[:, q, :]


def _peer_v(rows, w, x1r, tab, g, b, tb=256):
    T = rows.shape[0]
    expand = (np.arange(SLOTS)[:, None] == np.arange(G_ROWS16)[None, :] // 8).astype(np.float32)
    full = lambda a: pl.BlockSpec(a.shape, lambda i: (0, 0))
    return pl.pallas_call(
        _peer_v_body,
        grid=(T // tb,),
        in_specs=[pl.BlockSpec((tb, SLOTS), lambda i: (i, 0), memory_space=pltpu.SMEM),
                  pl.BlockSpec((tb, SLOTS), lambda i: (i, 0)),
                  pl.BlockSpec((tb, 8, LANES), lambda i: (i, 0, 0)),
                  full(tab), pl.BlockSpec((SLOTS, G_ROWS16), lambda i: (0, 0)),
                  pl.BlockSpec((8, G_ROWS16), lambda i: (0, 0)),
                  pl.BlockSpec((8, LANES), lambda i: (0, 0)),
                  pl.BlockSpec((8, LANES), lambda i: (0, 0))],
        out_specs=pl.BlockSpec((tb, D_MODEL), lambda i: (i, 0)),
        out_shape=jax.ShapeDtypeStruct((T, D_MODEL), f32),
        scratch_shapes=[pltpu.VMEM((TOKENS_PER_STEP_V, G_ROWS, LANES), i32), pltpu.VMEM((tb, G_ROWS16), f32),
                        pltpu.VMEM((tb, 8, LANES), f32)],
        compiler_params=_params("arbitrary", vmem=TABLE_VMEM_LIMIT),
        name="peer_v",
    )(rows, w, x1r, tab, jnp.asarray(expand, dtype=jnp.bfloat16), jnp.asarray(_chunk_mask()), g, b)


def _regroup_w_in(w):
    offs = np.cumsum(np.array(IN_SPLITS))[:-1].tolist()
    qa, ckv, qi, ki, wi, qb, fb, ib, gb, gate_a, gate_b = jnp.split(w, offs, axis=-1)
    pad = jnp.zeros((w.shape[0], LANES - IDX_DIM - IDX_HEADS), w.dtype)
    out = jnp.concatenate([qa, qi, qb, fb, ib, gb, gate_a, gate_b, ckv, ki, wi, pad], axis=-1)
    assert out.shape[-1] == PROJ_COLS
    return out.astype(bf16)


def kernel(x, w_in, kv_norm_g, w_uk, w_uv, rel_bias, lb_params, b_norm_g, w_br_a, w_br_b, w_o, ln1_g, ln1_b,
           w_pq, sub_keys1, sub_keys2, u_table, v_table, ln2_g, ln2_b):
    B, S, D = x.shape
    T = B * S
    lower_bounds = jnp.cumsum(jax.nn.softmax(lb_params.astype(f32), axis=0), axis=0)
    x2 = x.reshape(T, D)
    for l in range(DEPTH):
        proj = _project(x2, _regroup_w_in(w_in[l]))
        oa = _dsa(proj, kv_norm_g[l][None, :], w_uk[l], w_uv[l], rel_bias, B, S)
        ob = _hgrn(proj, lower_bounds[l][None, :], b_norm_g[l][None, :], B, S)
        x1r, ids, gate = _merge_route(oa, ob, proj, x2, w_br_a[l].astype(bf16), w_br_b[l].astype(bf16),
                                      w_o[l].astype(bf16), ln1_g[l][None, :], ln1_b[l][None, :],
                                      w_pq[l].astype(bf16), sub_keys1[l].astype(bf16), sub_keys2[l].astype(bf16))
        w = _peer_u(ids, x1r, gate, _pack_table(u_table[l]))
        x2 = _peer_v(ids, w, x1r, _pack_table(v_table[l]), ln2_g[l].reshape(8, LANES),
                     ln2_b[l].reshape(8, LANES)).reshape(T, D)
    return x2.reshape(B, S, D)
```

```python
import functools
import math

import numpy as np
import jax
import jax.numpy as jnp
from jax import lax
from jax.experimental import pallas as pl
from jax.experimental.pallas import tpu as pltpu

f32 = jnp.float32
bf16 = jnp.bfloat16
i32 = jnp.int32
HIGHEST = lax.Precision.HIGHEST

D_MODEL = 1024
DEPTH = 1
CHUNK = 64
Q_BLOCK = 128
A_HEADS = 8
A_HEAD_DIM = 64
A_Q = A_HEADS * A_HEAD_DIM
A_LATENT = 128
IDX_HEADS = 8
IDX_DIM = 64
TOPK_MAX = 256
REL_BUCKETS = 32
REL_MAX_DIST = 128
B_HEADS = 4
B_KEY_DIM = 128
B_VAL_DIM = 128
B_K = B_HEADS * B_KEY_DIM
B_V = B_HEADS * B_VAL_DIM
PEER_HEADS = 8
PEER_KEYS = 128
PEER_QDIM = 256
PEER_TOPK = 16
ALPHA = (2 * DEPTH) ** 0.25
LN_EPS = 1e-5
RMS_EPS = 1e-6

IN_SPLITS = (A_Q, A_LATENT, IDX_HEADS * IDX_DIM, IDX_DIM, IDX_HEADS, B_K, B_K, B_V, B_V, D_MODEL, D_MODEL)

COL_QA, COL_QI, COL_QB, COL_FB, COL_IB, COL_GB = 0, 512, 1024, 1536, 2048, 2560
COL_GATE_A, COL_GATE_B, COL_CKV, COL_KIWI = 3072, 4096, 5120, 5248
PROJ_COLS = 5376
LANES = 128
ROWS_PER_EXPERT = 4

INT_MIN = -(2 ** 31)
NEG_BIG = -1e30
VMEM_LIMIT = 48 * 1024 * 1024
TABLE_VMEM_LIMIT = 56 * 1024 * 1024

_NT = (((1,), (1,)), ((), ()))


def _params(*sem, vmem=VMEM_LIMIT):
    return pltpu.CompilerParams(dimension_semantics=sem, vmem_limit_bytes=vmem)


def _proj_body(x_ref, w_ref, o_ref):
    o_ref[...] = jnp.dot(x_ref[...].astype(bf16), w_ref[...], preferred_element_type=f32)


def _project(x2, w, tm=256):
    T = x2.shape[0]
    return pl.pallas_call(
        _proj_body,
        grid=(T // tm,),
        in_specs=[pl.BlockSpec((tm, D_MODEL), lambda i: (i, 0)),
                  pl.BlockSpec((D_MODEL, PROJ_COLS), lambda i: (0, 0))],
        out_specs=pl.BlockSpec((tm, PROJ_COLS), lambda i: (i, 0)),
        out_shape=jax.ShapeDtypeStruct((T, PROJ_COLS), f32),
        compiler_params=_params("arbitrary"),
        name="project",
    )(x2, w)


def _hgrn_body(q_ref, f_ref, i_ref, g_ref, lb_ref, ng_ref, o_ref, st_ref, *, n_chunks):
    @pl.when(pl.program_id(1) == 0)
    def _():
        st_ref[...] = jnp.zeros_like(st_ref)

    lb = lb_ref[...]
    ng = ng_ref[...]
    row = lax.broadcasted_iota(i32, (CHUNK, CHUNK), 0)
    col = lax.broadcasted_iota(i32, (CHUNK, CHUNK), 1)
    causal = row >= col
    tri = causal.astype(f32)
    for n in range(n_chunks):
        sl = pl.ds(n * CHUNK, CHUNK)
        z = f_ref[sl, :]
        log_f = jnp.log(lb + (1.0 - lb) * jax.nn.sigmoid(z))
        kk = (1.0 - lb) * jax.nn.sigmoid(-z)
        bc = jnp.dot(tri, log_f, precision=HIGHEST, preferred_element_type=f32)
        b_mid = bc[CHUNK // 2 - 1:CHUNK // 2, :]
        b_last = bc[CHUNK - 1:CHUNK, :]
        q = q_ref[sl, :]
        v = i_ref[sl, :]
        g = g_ref[sl, :]
        q_in = (q * jnp.exp(bc - b_mid)).astype(bf16)
        k_in = (kk * jnp.exp(b_mid - bc)).astype(bf16)
        k_dec = kk * jnp.exp(b_last - bc)
        q_dec = (q * jnp.exp(bc)).astype(bf16)
        decay = jnp.exp(b_last)
        outs = []
        for h in range(B_HEADS):
            hs = slice(h * B_KEY_DIM, (h + 1) * B_KEY_DIM)
            attn = lax.dot_general(q_in[:, hs], k_in[:, hs], _NT, preferred_element_type=f32)
            attn = jnp.where(causal, attn, 0.0)
            vh = v[:, hs]
            o = jnp.dot(attn.astype(bf16), vh.astype(bf16), preferred_element_type=f32)
            st = st_ref[h]
            o = o + lax.dot_general(q_dec[:, hs], st.astype(bf16), _NT, preferred_element_type=f32)
            d_st = jnp.dot(vh.T.astype(bf16), k_dec[:, hs].astype(bf16), preferred_element_type=f32)
            st_ref[h] = decay[:, hs] * st + d_st
            ms = jnp.mean(o * o, axis=-1, keepdims=True)
            gh = g[:, hs]
            outs.append(o * lax.rsqrt(ms + RMS_EPS) * ng * gh * jax.nn.sigmoid(gh))
        o_ref[sl, :] = jnp.concatenate(outs, axis=1)


def _hgrn(proj, lb, norm_g, B, S, tc=256):
    nc = S // tc
    blk = lambda c: pl.BlockSpec((tc, B_K), lambda b, s, c=c: (b * nc + s, c))
    return pl.pallas_call(
        functools.partial(_hgrn_body, n_chunks=tc // CHUNK),
        grid=(B, nc),
        in_specs=[blk(COL_QB // B_K), blk(COL_FB // B_K), blk(COL_IB // B_K), blk(COL_GB // B_K),
                  pl.BlockSpec((1, B_K), lambda b, s: (0, 0)),
                  pl.BlockSpec((1, B_VAL_DIM), lambda b, s: (0, 0))],
        out_specs=pl.BlockSpec((tc, B_V), lambda b, s: (b * nc + s, 0)),
        out_shape=jax.ShapeDtypeStruct((B * S, B_V), f32),
        scratch_shapes=[pltpu.VMEM((B_HEADS, B_VAL_DIM, B_KEY_DIM), f32)],
        compiler_params=_params("arbitrary", "arbitrary"),
        name="hgrn",
    )(proj, proj, proj, proj, lb, norm_g)


def _t5_bucket(rel):
    half = REL_BUCKETS // 2
    max_exact = half // 2
    base = jnp.where(rel > 0, half, 0)
    n = jnp.abs(rel)
    nf = jnp.maximum(n, 1).astype(f32)
    large = max_exact + (jnp.log(nf / max_exact) / math.log(REL_MAX_DIST / max_exact) * (half - max_exact)).astype(i32)
    large = jnp.minimum(large, half - 1)
    return base + jnp.where(n < max_exact, n, large)


def _paired(n, fn, init=None):
    def pair(i, c):
        return fn(2 * i + 1, fn(2 * i, c))
    c = lax.fori_loop(0, lax.shift_right_logical(n, 1), pair, init)
    return lax.cond((n & 1) == 1, lambda c: fn(n - 1, c), lambda c: c, c)


def _dsa_body(qa_ref, qi_ref, kwq_ref, ckv_ref, kw_ref, g_ref, wuk_ref, wuvt_ref, btt_ref, bfar_ref, o_ref,
                ckvn_ref, ckvnt_ref, skey_ref, s_ref, mt_ref, lt_ref, acc_ref, qlat_ref, qih_ref, wqt_ref,
                mrow_ref, thr_ref, ilim_ref, *, S, n_sel):
    j = pl.program_id(1)
    QB = Q_BLOCK
    n_kb = S // QB
    scale = A_HEAD_DIM ** -0.5
    keyi = lax.broadcasted_iota(i32, (QB, QB), 0)
    qryi = lax.broadcasted_iota(i32, (QB, QB), 1)

    def rows(kb):
        return pl.ds(pl.multiple_of(kb * QB, QB), QB)

    @pl.when(j == 0)
    def _():
        def norm_blk(kb, c):
            cb = ckv_ref[rows(kb), :]
            ms = jnp.mean(cb * cb, axis=-1, keepdims=True)
            cn = cb * lax.rsqrt(ms + RMS_EPS) * g_ref[...]
            ckvn_ref[rows(kb), :] = cn.astype(bf16)
            ckvnt_ref[kb] = cn.T.astype(bf16)
            return c
        lax.fori_loop(0, n_kb, norm_blk, 0)

    qa = qa_ref[...]
    qi = qi_ref[...]
    wqt_ref[...] = kwq_ref[...].T[IDX_DIM:IDX_DIM + IDX_HEADS, :] * ((IDX_HEADS * IDX_DIM) ** -0.5)
    for h in range(A_HEADS):
        qh = qa[:, h * A_HEAD_DIM:(h + 1) * A_HEAD_DIM].astype(bf16)
        qlat_ref[h * QB:(h + 1) * QB, :] = jnp.dot(qh, wuk_ref[h], preferred_element_type=f32).astype(bf16)
    for h in range(IDX_HEADS):
        qih_ref[h * QB:(h + 1) * QB, :] = qi[:, h * IDX_DIM:(h + 1) * IDX_DIM].astype(bf16)

    def index_block(kb, is_current):
        kblk = kw_ref[rows(kb), :][:, :IDX_DIM].astype(bf16)
        lg_all = lax.dot_general(kblk, qih_ref[...], _NT, preferred_element_type=f32)
        sc = jnp.zeros((QB, QB), f32)
        for h in range(IDX_HEADS):
            sc = sc + wqt_ref[h:h + 1, :] * jnp.maximum(lg_all[:, h * QB:(h + 1) * QB], 0.0)
        bits = lax.bitcast_convert_type(sc, i32)
        key = bits ^ ((bits >> 31) & jnp.int32(0x7FFFFFFF))
        if is_current:
            key = jnp.where((qryi < CHUNK) & (keyi >= CHUNK), jnp.int32(INT_MIN), key)
        skey_ref[kb] = key

    def idx_loop(kb, c):
        index_block(kb, False)
        return c
    _paired(j, idx_loop)
    index_block(j, True)

    def count_where(pred_fn):
        def body(kb, acc):
            return acc + jnp.where(pred_fn(kb, skey_ref[kb]), 1.0, 0.0)
        acc = _paired(j + 1, body, jnp.zeros((QB, QB), f32))
        return jnp.sum(acc, axis=0, keepdims=True)

    thr_ref[...] = jnp.full((1, QB), INT_MIN, i32)
    ilim_ref[...] = jnp.full((1, QB), S, i32)
    kf = float(n_sel)

    @pl.when((j + 1) * QB > n_sel)
    def _():
        zero = jnp.zeros((1, QB), i32)
        c0 = count_where(lambda kb, key: key >= 0)
        n_all = jnp.zeros((1, QB), f32) + ((j + 1) * QB).astype(f32)
        thr = jnp.where(c0 >= kf, zero, jnp.int32(INT_MIN))
        c_thr = jnp.where(c0 >= kf, c0, n_all)

        def bit_step(i, carry):
            thr, c_thr = carry
            cand = thr | jnp.left_shift(jnp.int32(1), 30 - i)
            c = count_where(lambda kb, key: key >= cand)
            ok = c >= kf
            return jnp.where(ok, cand, thr), jnp.where(ok, c, c_thr)
        thr, c_thr = lax.fori_loop(0, 31, bit_step, (thr, c_thr))
        thr_ref[...] = thr

        @pl.when(jnp.max(c_thr) > kf)
        def _():
            need = kf - count_where(lambda kb, key: key > thr)
            n_bits = int(math.ceil(math.log2(S + 1)))

            def lim_step(i, lim):
                cand = lim + jnp.left_shift(jnp.int32(1), n_bits - 1 - i)
                c = count_where(lambda kb, key: (key == thr) & (kb * QB + keyi < cand))
                return jnp.where(c <= need, cand, lim)
            ilim_ref[...] = lax.fori_loop(0, n_bits, lim_step, zero)

    mt_ref[...] = jnp.full(mt_ref.shape, NEG_BIG, f32)
    thr = thr_ref[...]
    lim = ilim_ref[...]

    def score_block(kb, bias_of_head):
        key = skey_ref[kb]
        sel = ((key > thr) | ((key == thr) & (kb * QB + keyi < lim))) & (key > jnp.int32(INT_MIN))
        s_all = lax.dot_general(ckvn_ref[rows(kb), :], qlat_ref[...], _NT, preferred_element_type=f32)
        for h in range(A_HEADS):
            s = s_all[:, h * QB:(h + 1) * QB] * scale + bias_of_head(h)
            s = jnp.where(sel, s, NEG_BIG)
            s_ref[kb, h] = s
            mt_ref[h] = jnp.maximum(mt_ref[h], s)

    def far_loop(kb, c):
        score_block(kb, lambda h: bfar_ref[h:h + 1, :])
        return c
    _paired(jnp.maximum(j - 1, 0), far_loop)

    @pl.when(j >= 1)
    def _():
        score_block(j - 1, lambda h: btt_ref[h, 0:QB, :])
    score_block(j, lambda h: btt_ref[h, QB:2 * QB, :])

    for h in range(A_HEADS):
        mrow_ref[h:h + 1, :] = jnp.max(mt_ref[h], axis=0, keepdims=True)
    lt_ref[...] = jnp.zeros(lt_ref.shape, f32)
    acc_ref[...] = jnp.zeros(acc_ref.shape, f32)

    def pv_block(kb, c):
        ps = []
        for h in range(A_HEADS):
            p = jnp.exp(s_ref[kb, h] - mrow_ref[h:h + 1, :])
            lt_ref[h] = lt_ref[h] + p
            ps.append(p.astype(bf16))
        acc_ref[...] = acc_ref[...] + jnp.dot(ckvnt_ref[kb], jnp.concatenate(ps, axis=1),
                                              preferred_element_type=f32)
        return c
    _paired(j + 1, pv_block)

    outs = []
    for h in range(A_HEADS):
        o_lat_t = acc_ref[:, h * QB:(h + 1) * QB] / jnp.sum(lt_ref[h], axis=0, keepdims=True)
        outs.append(jnp.dot(wuvt_ref[h], o_lat_t.astype(bf16), preferred_element_type=f32))
    o_ref[...] = jnp.concatenate(outs, axis=0).T


def _dsa(proj, kv_norm_g, w_uk, w_uv, rel_bias, B, S):
    QB = Q_BLOCK
    nq = S // QB
    n_sel = min(TOPK_MAX, S // 4)
    assert REL_MAX_DIST <= QB
    c = jnp.arange(2 * QB, dtype=i32)[:, None]
    r = jnp.arange(QB, dtype=i32)[None, :]
    onehot = jax.nn.one_hot(_t5_bucket(c - QB - r), REL_BUCKETS, dtype=f32)
    bias_tile_t = jnp.einsum('kqb,bh->hkq', onehot, rel_bias.astype(f32), precision=HIGHEST)
    bias_far = jnp.broadcast_to(rel_bias[_t5_bucket(jnp.int32(-QB - 1))][:, None], (A_HEADS, LANES)).astype(f32)
    wuk_t = jnp.transpose(w_uk, (1, 2, 0)).astype(bf16)
    wuv_t = jnp.transpose(w_uv, (1, 2, 0)).astype(bf16)
    qblk = lambda width, cb: pl.BlockSpec((QB, width), lambda b, j, cb=cb: (b * nq + j, cb))
    seq = lambda cb: pl.BlockSpec((S, LANES), lambda b, j, cb=cb: (b, cb))
    full = lambda a: pl.BlockSpec(a.shape, lambda b, j, n=a.ndim: (0,) * n)
    return pl.pallas_call(
        functools.partial(_dsa_body, S=S, n_sel=n_sel),
        grid=(B, nq),
        in_specs=[qblk(A_Q, COL_QA // A_Q), qblk(A_Q, COL_QI // A_Q), qblk(LANES, COL_KIWI // LANES),
                  seq(COL_CKV // LANES), seq(COL_KIWI // LANES),
                  full(kv_norm_g), full(wuk_t), full(wuv_t), full(bias_tile_t), full(bias_far)],
        out_specs=pl.BlockSpec((QB, A_Q), lambda b, j: (b * nq + j, 0)),
        out_shape=jax.ShapeDtypeStruct((B * S, A_Q), f32),
        scratch_shapes=[
            pltpu.VMEM((S, A_LATENT), bf16),
            pltpu.VMEM((S // QB, A_LATENT, QB), bf16),
            pltpu.VMEM((S // QB, QB, QB), i32),
            pltpu.VMEM((S // QB, A_HEADS, QB, QB), f32),
            pltpu.VMEM((A_HEADS, QB, QB), f32),
            pltpu.VMEM((A_HEADS, QB, QB), f32),
            pltpu.VMEM((A_LATENT, A_HEADS * QB), f32),
            pltpu.VMEM((A_HEADS * QB, A_LATENT), bf16),
            pltpu.VMEM((IDX_HEADS * QB, IDX_DIM), bf16),
            pltpu.VMEM((IDX_HEADS, QB), f32),
            pltpu.VMEM((A_HEADS, QB), f32),
            pltpu.VMEM((1, QB), i32),
            pltpu.VMEM((1, QB), i32),
        ],
        compiler_params=_params("arbitrary", "arbitrary"),
        name="dsa",
    )(proj, proj, proj, proj, proj, kv_norm_g, wuk_t, wuv_t, bias_tile_t, bias_far)


def _layer_norm(y, g, b):
    mu = jnp.mean(y, axis=-1, keepdims=True)
    var = jnp.mean(jnp.square(y - mu), axis=-1, keepdims=True)
    return (y - mu) * lax.rsqrt(var + LN_EPS) * g + b


def _merge_body(oa_ref, ob_ref, ga_ref, gb_ref, x_ref, wa_ref, wb_ref, wo_ref, g_ref, b_ref, o_ref, ot_ref):
    ya = jnp.dot(oa_ref[...].astype(bf16), wa_ref[...], preferred_element_type=f32)
    yb = jnp.dot(ob_ref[...].astype(bf16), wb_ref[...], preferred_element_type=f32)
    merged = jax.nn.sigmoid(ga_ref[...]) * ya + jax.nn.sigmoid(gb_ref[...]) * yb
    mix = jnp.dot(merged.astype(bf16), wo_ref[...], preferred_element_type=f32)
    y = _layer_norm(ALPHA * x_ref[...] + mix, g_ref[...], b_ref[...])
    o_ref[...] = y
    for q in range(D_MODEL // LANES):
        ot_ref[:, q, :] = y[:, q * LANES:(q + 1) * LANES]


def _merge(oa, ob, proj, x2, wa, wb, wo, g, b, tm=256):
    T = x2.shape[0]
    full = lambda a: pl.BlockSpec(a.shape, lambda i: (0, 0))
    return pl.pallas_call(
        _merge_body,
        grid=(T // tm,),
        in_specs=[pl.BlockSpec((tm, A_Q), lambda i: (i, 0)), pl.BlockSpec((tm, B_V), lambda i: (i, 0)),
                  pl.BlockSpec((tm, D_MODEL), lambda i: (i, COL_GATE_A // D_MODEL)),
                  pl.BlockSpec((tm, D_MODEL), lambda i: (i, COL_GATE_B // D_MODEL)),
                  pl.BlockSpec((tm, D_MODEL), lambda i: (i, 0)),
                  full(wa), full(wb), full(wo), full(g), full(b)],
        out_specs=[pl.BlockSpec((tm, D_MODEL), lambda i: (i, 0)),
                   pl.BlockSpec((tm, D_MODEL // LANES, LANES), lambda i: (i, 0, 0))],
        out_shape=[jax.ShapeDtypeStruct((T, D_MODEL), f32),
                   jax.ShapeDtypeStruct((T, D_MODEL // LANES, LANES), f32)],
        compiler_params=_params("arbitrary"),
        name="merge",
    )(oa, ob, proj, proj, x2, wa, wb, wo, g, b)


def _route_body(x_ref, wpq_ref, sk1_ref, sk2_ref, ids_ref, gate_ref,
                v1_ref, i1_ref, v2_ref, i2_ref, ts_ref, ti_ref, idt_ref, gt_ref):
    tb = x_ref.shape[0]
    K = PEER_TOPK
    q = jnp.dot(x_ref[...].astype(bf16), wpq_ref[...], preferred_element_type=f32)
    kiota = lax.broadcasted_iota(i32, (PEER_KEYS, tb), 0).astype(f32)
    r8 = lax.broadcasted_iota(i32, (8, tb), 0).astype(f32)
    r16 = lax.broadcasted_iota(i32, (16, tb), 0).astype(f32)
    flat = jnp.concatenate([r16] + [K * a + r8 for a in range(1, 8)] + [K * (8 + r8)], axis=0)
    half = PEER_QDIM // 2
    for h in range(PEER_HEADS):
        for part, (sk_ref, v_ref, i_ref) in enumerate(((sk1_ref, v1_ref, i1_ref), (sk2_ref, v2_ref, i2_ref))):
            off = h * PEER_QDIM + part * half
            qh = q[:, off:off + half].astype(bf16)
            s = lax.dot_general(sk_ref[...], qh, _NT, preferred_element_type=f32)
            for r in range(K):
                m = jnp.max(s, axis=0, keepdims=True)
                ix = jnp.min(jnp.where(s == m, kiota, float(PEER_KEYS)), axis=0, keepdims=True)
                v_ref[r:r + 1, :] = m
                i_ref[r:r + 1, :] = ix
                s = jnp.where(kiota == ix, -jnp.inf, s)
        v1, i1, v2, i2 = v1_ref[...], i1_ref[...], v2_ref[...], i2_ref[...]
        cand = [v1[0:1] + v2, ] + [v1[a:a + 1] + v2[0:8] for a in range(1, 8)] + [v1[8:16] + v2[0:1]]
        cid = [i1[0:1] * PEER_KEYS + i2, ] + [i1[a:a + 1] * PEER_KEYS + i2[0:8] for a in range(1, 8)] \
            + [i1[8:16] * PEER_KEYS + i2[0:1]]
        cand = jnp.concatenate(cand, axis=0)
        cid = jnp.concatenate(cid, axis=0)
        for r in range(K):
            m = jnp.max(cand, axis=0, keepdims=True)
            fsel = jnp.min(jnp.where(cand == m, flat, float(K * K)), axis=0, keepdims=True)
            hit = flat == fsel
            ts_ref[r:r + 1, :] = m
            ti_ref[r:r + 1, :] = jnp.sum(jnp.where(hit, cid, 0.0), axis=0, keepdims=True)
            cand = jnp.where(hit, -jnp.inf, cand)
        ts = ts_ref[...]
        e = jnp.exp(ts - jnp.max(ts, axis=0, keepdims=True))
        gt_ref[h * K:(h + 1) * K, :] = e / jnp.sum(e, axis=0, keepdims=True)
        idt_ref[h * K:(h + 1) * K, :] = (ti_ref[...] * ROWS_PER_EXPERT).astype(i32)
    gate_ref[...] = gt_ref[...].T
    ids_ref[...] = lax.bitcast_convert_type(lax.bitcast_convert_type(idt_ref[...], f32).T, i32)


def _route(x1, wpq, sk1, sk2, tb=256):
    T = x1.shape[0]
    HK = PEER_HEADS * PEER_TOPK
    full = lambda a: pl.BlockSpec(a.shape, lambda i: (0, 0))
    return pl.pallas_call(
        _route_body,
        grid=(T // tb,),
        in_specs=[pl.BlockSpec((tb, D_MODEL), lambda i: (i, 0)), full(wpq), full(sk1), full(sk2)],
        out_specs=[pl.BlockSpec((tb, HK), lambda i: (i, 0)), pl.BlockSpec((tb, HK), lambda i: (i, 0))],
        out_shape=[jax.ShapeDtypeStruct((T, HK), i32), jax.ShapeDtypeStruct((T, HK), f32)],
        scratch_shapes=[pltpu.VMEM((PEER_TOPK, tb), f32)] * 6 + [
                        pltpu.VMEM((HK, tb), i32), pltpu.VMEM((HK, tb), f32)],
        compiler_params=_params("arbitrary"),
        name="route",
    )(x1, wpq, sk1, sk2)


SLOTS = PEER_HEADS * PEER_TOPK
G_ROWS = SLOTS * ROWS_PER_EXPERT
G_ROWS16 = 2 * G_ROWS
TOKENS_PER_STEP_U = 4
TOKENS_PER_STEP_V = 16


def _pack_table(t):
    E, D = t.shape
    rb = 512
    R = ROWS_PER_EXPERT

    def body(t_ref, o_ref):
        x = t_ref[...]
        for i in range(R):
            lo = lax.bitcast_convert_type(x[:, (2 * i) * LANES:(2 * i + 1) * LANES].astype(jnp.bfloat16).astype(f32), i32)
            hi = lax.bitcast_convert_type(x[:, (2 * i + 1) * LANES:(2 * i + 2) * LANES].astype(jnp.bfloat16).astype(f32), i32)
            o_ref[pl.ds(i, rb, stride=R), :] = (hi & jnp.int32(-65536)) | lax.shift_right_logical(lo, 16)

    return pl.pallas_call(
        body,
        grid=(E // rb,),
        in_specs=[pl.BlockSpec((rb, D), lambda i: (i, 0))],
        out_specs=pl.BlockSpec((rb * R, LANES), lambda i: (i, 0)),
        out_shape=jax.ShapeDtypeStruct((E * R, LANES), i32),
        compiler_params=_params("arbitrary"),
        name="pack_table",
    )(t)


def _gather_rows(rows_ref, t, tab_ref, g_ref):
    R = ROWS_PER_EXPERT
    for k in range(SLOTS):
        r0 = rows_ref[t, k]
        g_ref[R * k:R * (k + 1), :] = tab_ref[pl.ds(pl.multiple_of(r0, R), R), :]


def _split2(a):
    hi = a.astype(bf16)
    return jnp.concatenate([hi, (a - hi.astype(f32)).astype(bf16)], axis=0)


def _dot_select(a, sel_ref):
    hi = a.astype(bf16)
    lo = (a - hi.astype(f32)).astype(bf16)
    return (jnp.dot(hi, sel_ref[...], preferred_element_type=f32)
            + jnp.dot(lo, sel_ref[...], preferred_element_type=f32))


def _token_pipeline(tb, rows_ref, tab_ref, g_scr, issue, finish):
    per_step = g_scr.shape[0]
    for u in range(per_step):
        _gather_rows(rows_ref, u, tab_ref, g_scr.at[u])

    def step(i, c):
        for u in range(per_step):
            t = i * per_step + u
            z = issue(t, pltpu.bitcast(g_scr[u], jnp.bfloat16))
            _gather_rows(rows_ref, jnp.minimum(t + per_step, tb - 1), tab_ref, g_scr.at[u])
            finish(t, z)
        return c
    lax.fori_loop(0, tb // per_step, step, 0)


def _peer_u_body(rows_ref, x_ref, gate_ref, tab_ref, mask_ref, fold_ref, w_ref, g_scr, r_scr):
    tb = x_ref.shape[0]

    def issue(t, gb):
        return lax.dot_general(_split2(x_ref[t]), gb, _NT, preferred_element_type=f32)

    def finish(t, z):
        r_scr[pl.ds(t, 1), :] = jnp.sum((z[0:8] + z[8:16]) * mask_ref[...], axis=0, keepdims=True)

    _token_pipeline(tb, rows_ref, tab_ref, g_scr, issue, finish)
    h = _dot_select(r_scr[...], fold_ref)
    w_ref[...] = gate_ref[...] * (0.5 * h * (1.0 + lax.erf(h * (2.0 ** -0.5))))


def _chunk_mask():
    return (np.arange(G_ROWS16)[None, :] % 8 == np.arange(8)[:, None]).astype(np.float32)


def _peer_u(rows, x1r, gate, tab, tb=256):
    T = rows.shape[0]
    fold = (np.arange(G_ROWS16)[:, None] // 8 == np.arange(SLOTS)[None, :]).astype(np.float32)
    full = lambda a: pl.BlockSpec(a.shape, lambda i: (0, 0))
    return pl.pallas_call(
        _peer_u_body,
        grid=(T // tb,),
        in_specs=[pl.BlockSpec((tb, SLOTS), lambda i: (i, 0), memory_space=pltpu.SMEM),
                  pl.BlockSpec((tb, 8, LANES), lambda i: (i, 0, 0)),
                  pl.BlockSpec((tb, SLOTS), lambda i: (i, 0)),
                  full(tab), pl.BlockSpec((8, G_ROWS16), lambda i: (0, 0)),
                  pl.BlockSpec((G_ROWS16, SLOTS), lambda i: (0, 0))],
        out_specs=pl.BlockSpec((tb, SLOTS), lambda i: (i, 0)),
        out_shape=jax.ShapeDtypeStruct((T, SLOTS), f32),
        scratch_shapes=[pltpu.VMEM((TOKENS_PER_STEP_U, G_ROWS, LANES), i32), pltpu.VMEM((tb, G_ROWS16), f32)],
        compiler_params=_params("arbitrary", vmem=TABLE_VMEM_LIMIT),
        name="peer_u",
    )(rows, x1r, gate, tab, jnp.asarray(_chunk_mask()), jnp.asarray(fold, dtype=jnp.bfloat16))


def _peer_v_body(rows_ref, w_ref, x_ref, tab_ref, exp_ref, mask_ref, g_ref, b_ref, o_ref, g_scr, wexp_scr, y_scr):
    tb = x_ref.shape[0]
    wexp_scr[...] = _dot_select(w_ref[...], exp_ref)

    def issue(t, gb):
        return jnp.dot(_split2(mask_ref[...] * wexp_scr[pl.ds(t, 1), :]), gb, preferred_element_type=f32)

    def finish(t, z):
        y_scr[t] = z[0:8] + z[8:16]

    _token_pipeline(tb, rows_ref, tab_ref, g_scr, issue, finish)
    y = ALPHA * x_ref[...] + y_scr[...]
    n_el = float(D_MODEL)
    mu = jnp.sum(y, axis=(1, 2), keepdims=True) / n_el
    yc = y - mu
    var = jnp.sum(yc * yc, axis=(1, 2), keepdims=True) / n_el
    y_scr[...] = yc * lax.rsqrt(var + LN_EPS) * g_ref[...] + b_ref[...]
    for q in range(D_MODEL // LANES):
        o_ref[:, q * LANES:(q + 1) * LANES] = y_scr[:, q, :]


def _peer_v(rows, w, x1r, tab, g, b, tb=512):
    T = rows.shape[0]
    expand = (np.arange(SLOTS)[:, None] == np.arange(G_ROWS16)[None, :] // 8).astype(np.float32)
    full = lambda a: pl.BlockSpec(a.shape, lambda i: (0, 0))
    return pl.pallas_call(
        _peer_v_body,
        grid=(T // tb,),
        in_specs=[pl.BlockSpec((tb, SLOTS), lambda i: (i, 0), memory_space=pltpu.SMEM),
                  pl.BlockSpec((tb, SLOTS), lambda i: (i, 0)),
                  pl.BlockSpec((tb, 8, LANES), lambda i: (i, 0, 0)),
                  full(tab), pl.BlockSpec((SLOTS, G_ROWS16), lambda i: (0, 0)),
                  pl.BlockSpec((8, G_ROWS16), lambda i: (0, 0)),
                  pl.BlockSpec((8, LANES), lambda i: (0, 0)),
                  pl.BlockSpec((8, LANES), lambda i: (0, 0))],
        out_specs=pl.BlockSpec((tb, D_MODEL), lambda i: (i, 0)),
        out_shape=jax.ShapeDtypeStruct((T, D_MODEL), f32),
        scratch_shapes=[pltpu.VMEM((TOKENS_PER_STEP_V, G_ROWS, LANES), i32), pltpu.VMEM((tb, G_ROWS16), f32),
                        pltpu.VMEM((tb, 8, LANES), f32)],
        compiler_params=_params("arbitrary", vmem=TABLE_VMEM_LIMIT),
        name="peer_v",
    )(rows, w, x1r, tab, jnp.asarray(expand, dtype=jnp.bfloat16), jnp.asarray(_chunk_mask()), g, b)


def _regroup_w_in(w):
    offs = np.cumsum(np.array(IN_SPLITS))[:-1].tolist()
    qa, ckv, qi, ki, wi, qb, fb, ib, gb, gate_a, gate_b = jnp.split(w, offs, axis=-1)
    pad = jnp.zeros((w.shape[0], LANES - IDX_DIM - IDX_HEADS), w.dtype)
    out = jnp.concatenate([qa, qi, qb, fb, ib, gb, gate_a, gate_b, ckv, ki, wi, pad], axis=-1)
    assert out.shape[-1] == PROJ_COLS
    return out.astype(bf16)


def kernel(x, w_in, kv_norm_g, w_uk, w_uv, rel_bias, lb_params, b_norm_g, w_br_a, w_br_b, w_o, ln1_g, ln1_b,
           w_pq, sub_keys1, sub_keys2, u_table, v_table, ln2_g, ln2_b):
    B, S, D = x.shape
    T = B * S
    lower_bounds = jnp.cumsum(jax.nn.softmax(lb_params.astype(f32), axis=0), axis=0)
    x2 = x.reshape(T, D)
    for l in range(DEPTH):
        proj = _project(x2, _regroup_w_in(w_in[l]))
        oa = _dsa(proj, kv_norm_g[l][None, :], w_uk[l], w_uv[l], rel_bias, B, S)
        ob = _hgrn(proj, lower_bounds[l][None, :], b_norm_g[l][None, :], B, S)
        x1, x1r = _merge(oa, ob, proj, x2, w_br_a[l].astype(bf16), w_br_b[l].astype(bf16), w_o[l].astype(bf16),
                         ln1_g[l][None, :], ln1_b[l][None, :])
        ids, gate = _route(x1, w_pq[l].astype(bf16), sub_keys1[l].astype(bf16), sub_keys2[l].astype(bf16))
        w = _peer_u(ids, x1r, gate, _pack_table(u_table[l]))
        x2 = _peer_v(ids, w, x1r, _pack_table(v_table[l]), ln2_g[l].reshape(8, LANES),
                     ln2_b[l].reshape(8, LANES)).reshape(T, D)
    return x2.reshape(B, S, D)
```

```python
import functools
import math

import numpy as np
import jax
import jax.numpy as jnp
from jax import lax
from jax.experimental import pallas as pl
from jax.experimental.pallas import tpu as pltpu

f32 = jnp.float32
bf16 = jnp.bfloat16
i32 = jnp.int32
HIGHEST = lax.Precision.HIGHEST

D_MODEL = 1024
DEPTH = 1
CHUNK = 64
Q_BLOCK = 128
A_HEADS = 8
A_HEAD_DIM = 64
A_Q = A_HEADS * A_HEAD_DIM
A_LATENT = 128
IDX_HEADS = 8
IDX_DIM = 64
TOPK_MAX = 256
REL_BUCKETS = 32
REL_MAX_DIST = 128
B_HEADS = 4
B_KEY_DIM = 128
B_VAL_DIM = 128
B_K = B_HEADS * B_KEY_DIM
B_V = B_HEADS * B_VAL_DIM
PEER_HEADS = 8
PEER_KEYS = 128
PEER_QDIM = 256
PEER_TOPK = 16
ALPHA = (2 * DEPTH) ** 0.25
LN_EPS = 1e-5
RMS_EPS = 1e-6

IN_SPLITS = (A_Q, A_LATENT, IDX_HEADS * IDX_DIM, IDX_DIM, IDX_HEADS, B_K, B_K, B_V, B_V, D_MODEL, D_MODEL)

COL_QA, COL_QI, COL_QB, COL_FB, COL_IB, COL_GB = 0, 512, 1024, 1536, 2048, 2560
COL_GATE_A, COL_GATE_B, COL_CKV, COL_KIWI = 3072, 4096, 5120, 5248
PROJ_COLS = 5376
LANES = 128
ROWS_PER_EXPERT = 4

INT_MIN = -(2 ** 31)
NEG_BIG = -1e30
VMEM_LIMIT = 48 * 1024 * 1024
TABLE_VMEM_LIMIT = 56 * 1024 * 1024

_NT = (((1,), (1,)), ((), ()))


def _params(*sem, vmem=VMEM_LIMIT):
    return pltpu.CompilerParams(dimension_semantics=sem, vmem_limit_bytes=vmem)


def _proj_body(x_ref, w_ref, o_ref):
    o_ref[...] = jnp.dot(x_ref[...].astype(bf16), w_ref[...], preferred_element_type=f32)


def _project(x2, w, tm=256):
    T = x2.shape[0]
    return pl.pallas_call(
        _proj_body,
        grid=(T // tm,),
        in_specs=[pl.BlockSpec((tm, D_MODEL), lambda i: (i, 0)),
                  pl.BlockSpec((D_MODEL, PROJ_COLS), lambda i: (0, 0))],
        out_specs=pl.BlockSpec((tm, PROJ_COLS), lambda i: (i, 0)),
        out_shape=jax.ShapeDtypeStruct((T, PROJ_COLS), f32),
        compiler_params=_params("arbitrary"),
        name="project",
    )(x2, w)


def _hgrn_body(q_ref, f_ref, i_ref, g_ref, lb_ref, ng_ref, o_ref, st_ref, *, n_chunks):
    @pl.when(pl.program_id(1) == 0)
    def _():
        st_ref[...] = jnp.zeros_like(st_ref)

    lb = lb_ref[...]
    ng = ng_ref[...]
    row = lax.broadcasted_iota(i32, (CHUNK, CHUNK), 0)
    col = lax.broadcasted_iota(i32, (CHUNK, CHUNK), 1)
    causal = row >= col
    tri = causal.astype(f32)
    for n in range(n_chunks):
        sl = pl.ds(n * CHUNK, CHUNK)
        z = f_ref[sl, :]
        log_f = jnp.log(lb + (1.0 - lb) * jax.nn.sigmoid(z))
        kk = (1.0 - lb) * jax.nn.sigmoid(-z)
        bc = jnp.dot(tri, log_f, precision=HIGHEST, preferred_element_type=f32)
        b_mid = bc[CHUNK // 2 - 1:CHUNK // 2, :]
        b_last = bc[CHUNK - 1:CHUNK, :]
        q = q_ref[sl, :]
        v = i_ref[sl, :]
        g = g_ref[sl, :]
        q_in = (q * jnp.exp(bc - b_mid)).astype(bf16)
        k_in = (kk * jnp.exp(b_mid - bc)).astype(bf16)
        k_dec = kk * jnp.exp(b_last - bc)
        q_dec = (q * jnp.exp(bc)).astype(bf16)
        decay = jnp.exp(b_last)
        outs = []
        for h in range(B_HEADS):
            hs = slice(h * B_KEY_DIM, (h + 1) * B_KEY_DIM)
            attn = lax.dot_general(q_in[:, hs], k_in[:, hs], _NT, preferred_element_type=f32)
            attn = jnp.where(causal, attn, 0.0)
            vh = v[:, hs]
            o = jnp.dot(attn.astype(bf16), vh.astype(bf16), preferred_element_type=f32)
            st = st_ref[h]
            o = o + lax.dot_general(q_dec[:, hs], st.astype(bf16), _NT, preferred_element_type=f32)
            d_st = jnp.dot(vh.T.astype(bf16), k_dec[:, hs].astype(bf16), preferred_element_type=f32)
            st_ref[h] = decay[:, hs] * st + d_st
            ms = jnp.mean(o * o, axis=-1, keepdims=True)
            gh = g[:, hs]
            outs.append(o * lax.rsqrt(ms + RMS_EPS) * ng * gh * jax.nn.sigmoid(gh))
        o_ref[sl, :] = jnp.concatenate(outs, axis=1)


def _hgrn(proj, lb, norm_g, B, S, tc=256):
    nc = S // tc
    blk = lambda c: pl.BlockSpec((tc, B_K), lambda b, s, c=c: (b * nc + s, c))
    return pl.pallas_call(
        functools.partial(_hgrn_body, n_chunks=tc // CHUNK),
        grid=(B, nc),
        in_specs=[blk(COL_QB // B_K), blk(COL_FB // B_K), blk(COL_IB // B_K), blk(COL_GB // B_K),
                  pl.BlockSpec((1, B_K), lambda b, s: (0, 0)),
                  pl.BlockSpec((1, B_VAL_DIM), lambda b, s: (0, 0))],
        out_specs=pl.BlockSpec((tc, B_V), lambda b, s: (b * nc + s, 0)),
        out_shape=jax.ShapeDtypeStruct((B * S, B_V), f32),
        scratch_shapes=[pltpu.VMEM((B_HEADS, B_VAL_DIM, B_KEY_DIM), f32)],
        compiler_params=_params("arbitrary", "arbitrary"),
        name="hgrn",
    )(proj, proj, proj, proj, lb, norm_g)


def _t5_bucket(rel):
    half = REL_BUCKETS // 2
    max_exact = half // 2
    base = jnp.where(rel > 0, half, 0)
    n = jnp.abs(rel)
    nf = jnp.maximum(n, 1).astype(f32)
    large = max_exact + (jnp.log(nf / max_exact) / math.log(REL_MAX_DIST / max_exact) * (half - max_exact)).astype(i32)
    large = jnp.minimum(large, half - 1)
    return base + jnp.where(n < max_exact, n, large)


def _paired(n, fn, init=None, quad=True):
    if not quad:
        def pair(i, c):
            return fn(2 * i + 1, fn(2 * i, c))
        c = lax.fori_loop(0, lax.shift_right_logical(n, 1), pair, init)
        return lax.cond((n & 1) == 1, lambda c: fn(n - 1, c), lambda c: c, c)

    def four(i, c):
        return fn(4 * i + 3, fn(4 * i + 2, fn(4 * i + 1, fn(4 * i, c))))
    c = lax.fori_loop(0, lax.shift_right_logical(n, 2), four, init)
    base = n & ~3
    c = lax.cond((n & 2) == 2, lambda c: fn(base + 1, fn(base, c)), lambda c: c, c)
    return lax.cond((n & 1) == 1, lambda c: fn(n - 1, c), lambda c: c, c)


def _dsa_body(qa_ref, qi_ref, kwq_ref, ckv_ref, kw_ref, g_ref, wuk_ref, wuvt_ref, btt_ref, bfar_ref, o_ref,
                ckvn_ref, ckvnt_ref, skey_ref, s_ref, mt_ref, lt_ref, acc_ref, qlat_ref, qih_ref, wqt_ref,
                mrow_ref, thr_ref, ilim_ref, *, S, n_sel):
    j = pl.program_id(1)
    QB = Q_BLOCK
    n_kb = S // QB
    scale = A_HEAD_DIM ** -0.5
    keyi = lax.broadcasted_iota(i32, (QB, QB), 0)
    qryi = lax.broadcasted_iota(i32, (QB, QB), 1)

    def rows(kb):
        return pl.ds(pl.multiple_of(kb * QB, QB), QB)

    @pl.when(j == 0)
    def _():
        def norm_blk(kb, c):
            cb = ckv_ref[rows(kb), :]
            ms = jnp.mean(cb * cb, axis=-1, keepdims=True)
            cn = cb * lax.rsqrt(ms + RMS_EPS) * g_ref[...]
            ckvn_ref[rows(kb), :] = cn.astype(bf16)
            ckvnt_ref[kb] = cn.T.astype(bf16)
            return c
        lax.fori_loop(0, n_kb, norm_blk, 0)

    qa = qa_ref[...]
    qi = qi_ref[...]
    wqt_ref[...] = kwq_ref[...].T[IDX_DIM:IDX_DIM + IDX_HEADS, :] * ((IDX_HEADS * IDX_DIM) ** -0.5)
    for h in range(A_HEADS):
        qh = qa[:, h * A_HEAD_DIM:(h + 1) * A_HEAD_DIM].astype(bf16)
        qlat_ref[h * QB:(h + 1) * QB, :] = jnp.dot(qh, wuk_ref[h], preferred_element_type=f32).astype(bf16)
    for h in range(IDX_HEADS):
        qih_ref[h * QB:(h + 1) * QB, :] = qi[:, h * IDX_DIM:(h + 1) * IDX_DIM].astype(bf16)

    def index_block(kb, is_current):
        kblk = kw_ref[rows(kb), :][:, :IDX_DIM].astype(bf16)
        lg_all = lax.dot_general(kblk, qih_ref[...], _NT, preferred_element_type=f32)
        sc = jnp.zeros((QB, QB), f32)
        for h in range(IDX_HEADS):
            sc = sc + wqt_ref[h:h + 1, :] * jnp.maximum(lg_all[:, h * QB:(h + 1) * QB], 0.0)
        bits = lax.bitcast_convert_type(sc, i32)
        key = bits ^ ((bits >> 31) & jnp.int32(0x7FFFFFFF))
        if is_current:
            key = jnp.where((qryi < CHUNK) & (keyi >= CHUNK), jnp.int32(INT_MIN), key)
        skey_ref[kb] = key

    def idx_loop(kb, c):
        index_block(kb, False)
        return c
    _paired(j, idx_loop)
    index_block(j, True)

    def count_where(pred_fn):
        def body(kb, acc):
            return acc + jnp.where(pred_fn(kb, skey_ref[kb]), 1.0, 0.0)
        acc = _paired(j + 1, body, jnp.zeros((QB, QB), f32), quad=False)
        return jnp.sum(acc, axis=0, keepdims=True)

    thr_ref[...] = jnp.full((1, QB), INT_MIN, i32)
    ilim_ref[...] = jnp.full((1, QB), S, i32)
    kf = float(n_sel)

    @pl.when((j + 1) * QB > n_sel)
    def _():
        zero = jnp.zeros((1, QB), i32)
        c0 = count_where(lambda kb, key: key >= 0)
        n_all = jnp.zeros((1, QB), f32) + ((j + 1) * QB).astype(f32)
        thr = jnp.where(c0 >= kf, zero, jnp.int32(INT_MIN))
        c_thr = jnp.where(c0 >= kf, c0, n_all)

        def bit_step(i, carry):
            thr, c_thr = carry
            cand = thr | jnp.left_shift(jnp.int32(1), 30 - i)
            c = count_where(lambda kb, key: key >= cand)
            ok = c >= kf
            return jnp.where(ok, cand, thr), jnp.where(ok, c, c_thr)
        thr, c_thr = lax.fori_loop(0, 31, bit_step, (thr, c_thr))
        thr_ref[...] = thr

        @pl.when(jnp.max(c_thr) > kf)
        def _():
            need = kf - count_where(lambda kb, key: key > thr)
            n_bits = int(math.ceil(math.log2(S + 1)))

            def lim_step(i, lim):
                cand = lim + jnp.left_shift(jnp.int32(1), n_bits - 1 - i)
                c = count_where(lambda kb, key: (key == thr) & (kb * QB + keyi < cand))
                return jnp.where(c <= need, cand, lim)
            ilim_ref[...] = lax.fori_loop(0, n_bits, lim_step, zero)

    mt_ref[...] = jnp.full(mt_ref.shape, NEG_BIG, f32)
    thr = thr_ref[...]
    lim = ilim_ref[...]

    def score_block(kb, bias_of_head):
        key = skey_ref[kb]
        sel = ((key > thr) | ((key == thr) & (kb * QB + keyi < lim))) & (key > jnp.int32(INT_MIN))
        s_all = lax.dot_general(ckvn_ref[rows(kb), :], qlat_ref[...], _NT, preferred_element_type=f32)
        for h in range(A_HEADS):
            s = s_all[:, h * QB:(h + 1) * QB] * scale + bias_of_head(h)
            s = jnp.where(sel, s, NEG_BIG)
            s_ref[kb, h] = s
            mt_ref[h] = jnp.maximum(mt_ref[h], s)

    def far_loop(kb, c):
        score_block(kb, lambda h: bfar_ref[h:h + 1, :])
        return c
    _paired(jnp.maximum(j - 1, 0), far_loop)

    @pl.when(j >= 1)
    def _():
        score_block(j - 1, lambda h: btt_ref[h, 0:QB, :])
    score_block(j, lambda h: btt_ref[h, QB:2 * QB, :])

    for h in range(A_HEADS):
        mrow_ref[h:h + 1, :] = jnp.max(mt_ref[h], axis=0, keepdims=True)
    lt_ref[...] = jnp.zeros(lt_ref.shape, f32)
    acc_ref[...] = jnp.zeros(acc_ref.shape, f32)

    def pv_block(kb, c):
        ps = []
        for h in range(A_HEADS):
            p = jnp.exp(s_ref[kb, h] - mrow_ref[h:h + 1, :])
            lt_ref[h] = lt_ref[h] + p
            ps.append(p.astype(bf16))
        acc_ref[...] = acc_ref[...] + jnp.dot(ckvnt_ref[kb], jnp.concatenate(ps, axis=1),
                                              preferred_element_type=f32)
        return c
    _paired(j + 1, pv_block)

    outs = []
    for h in range(A_HEADS):
        o_lat_t = acc_ref[:, h * QB:(h + 1) * QB] / jnp.sum(lt_ref[h], axis=0, keepdims=True)
        outs.append(jnp.dot(wuvt_ref[h], o_lat_t.astype(bf16), preferred_element_type=f32))
    o_ref[...] = jnp.concatenate(outs, axis=0).T


def _dsa(proj, kv_norm_g, w_uk, w_uv, rel_bias, B, S):
    QB = Q_BLOCK
    nq = S // QB
    n_sel = min(TOPK_MAX, S // 4)
    assert REL_MAX_DIST <= QB
    c = jnp.arange(2 * QB, dtype=i32)[:, None]
    r = jnp.arange(QB, dtype=i32)[None, :]
    onehot = jax.nn.one_hot(_t5_bucket(c - QB - r), REL_BUCKETS, dtype=f32)
    bias_tile_t = jnp.einsum('kqb,bh->hkq', onehot, rel_bias.astype(f32), precision=HIGHEST)
    bias_far = jnp.broadcast_to(rel_bias[_t5_bucket(jnp.int32(-QB - 1))][:, None], (A_HEADS, LANES)).astype(f32)
    wuk_t = jnp.transpose(w_uk, (1, 2, 0)).astype(bf16)
    wuv_t = jnp.transpose(w_uv, (1, 2, 0)).astype(bf16)
    qblk = lambda width, cb: pl.BlockSpec((QB, width), lambda b, j, cb=cb: (b * nq + j, cb))
    seq = lambda cb: pl.BlockSpec((S, LANES), lambda b, j, cb=cb: (b, cb))
    full = lambda a: pl.BlockSpec(a.shape, lambda b, j, n=a.ndim: (0,) * n)
    return pl.pallas_call(
        functools.partial(_dsa_body, S=S, n_sel=n_sel),
        grid=(B, nq),
        in_specs=[qblk(A_Q, COL_QA // A_Q), qblk(A_Q, COL_QI // A_Q), qblk(LANES, COL_KIWI // LANES),
                  seq(COL_CKV // LANES), seq(COL_KIWI // LANES),
                  full(kv_norm_g), full(wuk_t), full(wuv_t), full(bias_tile_t), full(bias_far)],
        out_specs=pl.BlockSpec((QB, A_Q), lambda b, j: (b * nq + j, 0)),
        out_shape=jax.ShapeDtypeStruct((B * S, A_Q), f32),
        scratch_shapes=[
            pltpu.VMEM((S, A_LATENT), bf16),
            pltpu.VMEM((S // QB, A_LATENT, QB), bf16),
            pltpu.VMEM((S // QB, QB, QB), i32),
            pltpu.VMEM((S // QB, A_HEADS, QB, QB), f32),
            pltpu.VMEM((A_HEADS, QB, QB), f32),
            pltpu.VMEM((A_HEADS, QB, QB), f32),
            pltpu.VMEM((A_LATENT, A_HEADS * QB), f32),
            pltpu.VMEM((A_HEADS * QB, A_LATENT), bf16),
            pltpu.VMEM((IDX_HEADS * QB, IDX_DIM), bf16),
            pltpu.VMEM((IDX_HEADS, QB), f32),
            pltpu.VMEM((A_HEADS, QB), f32),
            pltpu.VMEM((1, QB), i32),
            pltpu.VMEM((1, QB), i32),
        ],
        compiler_params=_params("arbitrary", "arbitrary"),
        name="dsa",
    )(proj, proj, proj, proj, proj, kv_norm_g, wuk_t, wuv_t, bias_tile_t, bias_far)


def _layer_norm(y, g, b):
    mu = jnp.mean(y, axis=-1, keepdims=True)
    var = jnp.mean(jnp.square(y - mu), axis=-1, keepdims=True)
    return (y - mu) * lax.rsqrt(var + LN_EPS) * g + b


def _merge_body(oa_ref, ob_ref, ga_ref, gb_ref, x_ref, wa_ref, wb_ref, wo_ref, g_ref, b_ref, o_ref, ot_ref):
    ya = jnp.dot(oa_ref[...].astype(bf16), wa_ref[...], preferred_element_type=f32)
    yb = jnp.dot(ob_ref[...].astype(bf16), wb_ref[...], preferred_element_type=f32)
    merged = jax.nn.sigmoid(ga_ref[...]) * ya + jax.nn.sigmoid(gb_ref[...]) * yb
    mix = jnp.dot(merged.astype(bf16), wo_ref[...], preferred_element_type=f32)
    y = _layer_norm(ALPHA * x_ref[...] + mix, g_ref[...], b_ref[...])
    o_ref[...] = y
    for q in range(D_MODEL // LANES):
        ot_ref[:, q, :] = y[:, q * LANES:(q + 1) * LANES]


def _merge(oa, ob, proj, x2, wa, wb, wo, g, b, tm=256):
    T = x2.shape[0]
    full = lambda a: pl.BlockSpec(a.shape, lambda i: (0, 0))
    return pl.pallas_call(
        _merge_body,
        grid=(T // tm,),
        in_specs=[pl.BlockSpec((tm, A_Q), lambda i: (i, 0)), pl.BlockSpec((tm, B_V), lambda i: (i, 0)),
                  pl.BlockSpec((tm, D_MODEL), lambda i: (i, COL_GATE_A // D_MODEL)),
                  pl.BlockSpec((tm, D_MODEL), lambda i: (i, COL_GATE_B // D_MODEL)),
                  pl.BlockSpec((tm, D_MODEL), lambda i: (i, 0)),
                  full(wa), full(wb), full(wo), full(g), full(b)],
        out_specs=[pl.BlockSpec((tm, D_MODEL), lambda i: (i, 0)),
                   pl.BlockSpec((tm, D_MODEL // LANES, LANES), lambda i: (i, 0, 0))],
        out_shape=[jax.ShapeDtypeStruct((T, D_MODEL), f32),
                   jax.ShapeDtypeStruct((T, D_MODEL // LANES, LANES), f32)],
        compiler_params=_params("arbitrary"),
        name="merge",
    )(oa, ob, proj, proj, x2, wa, wb, wo, g, b)


def _route_body(x_ref, wpq_ref, sk1_ref, sk2_ref, ids_ref, gate_ref,
                v1_ref, i1_ref, v2_ref, i2_ref, ts_ref, ti_ref, idt_ref, gt_ref):
    tb = x_ref.shape[0]
    K = PEER_TOPK
    q = jnp.dot(x_ref[...].astype(bf16), wpq_ref[...], preferred_element_type=f32)
    kiota = lax.broadcasted_iota(i32, (PEER_KEYS, tb), 0).astype(f32)
    r8 = lax.broadcasted_iota(i32, (8, tb), 0).astype(f32)
    r16 = lax.broadcasted_iota(i32, (16, tb), 0).astype(f32)
    flat = jnp.concatenate([r16] + [K * a + r8 for a in range(1, 8)] + [K * (8 + r8)], axis=0)
    half = PEER_QDIM // 2
    for h in range(PEER_HEADS):
        for part, (sk_ref, v_ref, i_ref) in enumerate(((sk1_ref, v1_ref, i1_ref), (sk2_ref, v2_ref, i2_ref))):
            off = h * PEER_QDIM + part * half
            qh = q[:, off:off + half].astype(bf16)
            s = lax.dot_general(sk_ref[...], qh, _NT, preferred_element_type=f32)
            for r in range(K):
                m = jnp.max(s, axis=0, keepdims=True)
                ix = jnp.min(jnp.where(s == m, kiota, float(PEER_KEYS)), axis=0, keepdims=True)
                v_ref[r:r + 1, :] = m
                i_ref[r:r + 1, :] = ix
                s = jnp.where(kiota == ix, -jnp.inf, s)
        v1, i1, v2, i2 = v1_ref[...], i1_ref[...], v2_ref[...], i2_ref[...]
        cand = [v1[0:1] + v2, ] + [v1[a:a + 1] + v2[0:8] for a in range(1, 8)] + [v1[8:16] + v2[0:1]]
        cid = [i1[0:1] * PEER_KEYS + i2, ] + [i1[a:a + 1] * PEER_KEYS + i2[0:8] for a in range(1, 8)] \
            + [i1[8:16] * PEER_KEYS + i2[0:1]]
        cand = jnp.concatenate(cand, axis=0)
        cid = jnp.concatenate(cid, axis=0)
        for r in range(K):
            m = jnp.max(cand, axis=0, keepdims=True)
            fsel = jnp.min(jnp.where(cand == m, flat, float(K * K)), axis=0, keepdims=True)
            hit = flat == fsel
            ts_ref[r:r + 1, :] = m
            ti_ref[r:r + 1, :] = jnp.sum(jnp.where(hit, cid, 0.0), axis=0, keepdims=True)
            cand = jnp.where(hit, -jnp.inf, cand)
        ts = ts_ref[...]
        e = jnp.exp(ts - jnp.max(ts, axis=0, keepdims=True))
        gt_ref[h * K:(h + 1) * K, :] = e / jnp.sum(e, axis=0, keepdims=True)
        idt_ref[h * K:(h + 1) * K, :] = (ti_ref[...] * ROWS_PER_EXPERT).astype(i32)
    gate_ref[...] = gt_ref[...].T
    ids_ref[...] = lax.bitcast_convert_type(lax.bitcast_convert_type(idt_ref[...], f32).T, i32)


def _route(x1, wpq, sk1, sk2, tb=256):
    T = x1.shape[0]
    HK = PEER_HEADS * PEER_TOPK
    full = lambda a: pl.BlockSpec(a.shape, lambda i: (0, 0))
    return pl.pallas_call(
        _route_body,
        grid=(T // tb,),
        in_specs=[pl.BlockSpec((tb, D_MODEL), lambda i: (i, 0)), full(wpq), full(sk1), full(sk2)],
        out_specs=[pl.BlockSpec((tb, HK), lambda i: (i, 0)), pl.BlockSpec((tb, HK), lambda i: (i, 0))],
        out_shape=[jax.ShapeDtypeStruct((T, HK), i32), jax.ShapeDtypeStruct((T, HK), f32)],
        scratch_shapes=[pltpu.VMEM((PEER_TOPK, tb), f32)] * 6 + [
                        pltpu.VMEM((HK, tb), i32), pltpu.VMEM((HK, tb), f32)],
        compiler_params=_params("arbitrary"),
        name="route",
    )(x1, wpq, sk1, sk2)


SLOTS = PEER_HEADS * PEER_TOPK
G_ROWS = SLOTS * ROWS_PER_EXPERT
G_ROWS16 = 2 * G_ROWS
TOKENS_PER_STEP_U = 4
TOKENS_PER_STEP_V = 16


def _pack_table(t):
    E, D = t.shape
    rb = 512
    R = ROWS_PER_EXPERT

    def body(t_ref, o_ref):
        x = t_ref[...]
        for i in range(R):
            lo = lax.bitcast_convert_type(x[:, (2 * i) * LANES:(2 * i + 1) * LANES].astype(jnp.bfloat16).astype(f32), i32)
            hi = lax.bitcast_convert_type(x[:, (2 * i + 1) * LANES:(2 * i + 2) * LANES].astype(jnp.bfloat16).astype(f32), i32)
            o_ref[pl.ds(i, rb, stride=R), :] = (hi & jnp.int32(-65536)) | lax.shift_right_logical(lo, 16)

    return pl.pallas_call(
        body,
        grid=(E // rb,),
        in_specs=[pl.BlockSpec((rb, D), lambda i: (i, 0))],
        out_specs=pl.BlockSpec((rb * R, LANES), lambda i: (i, 0)),
        out_shape=jax.ShapeDtypeStruct((E * R, LANES), i32),
        compiler_params=_params("arbitrary"),
        name="pack_table",
    )(t)


def _gather_rows(rows_ref, t, tab_ref, g_ref):
    R = ROWS_PER_EXPERT
    for k in range(SLOTS):
        r0 = rows_ref[t, k]
        g_ref[R * k:R * (k + 1), :] = tab_ref[pl.ds(pl.multiple_of(r0, R), R), :]


def _split2(a):
    hi = a.astype(bf16)
    return jnp.concatenate([hi, (a - hi.astype(f32)).astype(bf16)], axis=0)


def _dot_select(a, sel_ref):
    hi = a.astype(bf16)
    lo = (a - hi.astype(f32)).astype(bf16)
    return (jnp.dot(hi, sel_ref[...], preferred_element_type=f32)
            + jnp.dot(lo, sel_ref[...], preferred_element_type=f32))


def _token_pipeline(tb, rows_ref, tab_ref, g_scr, issue, finish):
    per_step = g_scr.shape[0]
    for u in range(per_step):
        _gather_rows(rows_ref, u, tab_ref, g_scr.at[u])

    def step(i, c):
        for u in range(per_step):
            t = i * per_step + u
            z = issue(t, pltpu.bitcast(g_scr[u], jnp.bfloat16))
            _gather_rows(rows_ref, jnp.minimum(t + per_step, tb - 1), tab_ref, g_scr.at[u])
            finish(t, z)
        return c
    lax.fori_loop(0, tb // per_step, step, 0)


def _peer_u_body(rows_ref, x_ref, gate_ref, tab_ref, mask_ref, fold_ref, w_ref, g_scr, r_scr):
    tb = x_ref.shape[0]

    def issue(t, gb):
        return lax.dot_general(_split2(x_ref[t]), gb, _NT, preferred_element_type=f32)

    def finish(t, z):
        r_scr[pl.ds(t, 1), :] = jnp.sum((z[0:8] + z[8:16]) * mask_ref[...], axis=0, keepdims=True)

    _token_pipeline(tb, rows_ref, tab_ref, g_scr, issue, finish)
    h = _dot_select(r_scr[...], fold_ref)
    w_ref[...] = gate_ref[...] * (0.5 * h * (1.0 + lax.erf(h * (2.0 ** -0.5))))


def _chunk_mask():
    return (np.arange(G_ROWS16)[None, :] % 8 == np.arange(8)[:, None]).astype(np.float32)


def _peer_u(rows, x1r, gate, tab, tb=256):
    T = rows.shape[0]
    fold = (np.arange(G_ROWS16)[:, None] // 8 == np.arange(SLOTS)[None, :]).astype(np.float32)
    full = lambda a: pl.BlockSpec(a.shape, lambda i: (0, 0))
    return pl.pallas_call(
        _peer_u_body,
        grid=(T // tb,),
        in_specs=[pl.BlockSpec((tb, SLOTS), lambda i: (i, 0), memory_space=pltpu.SMEM),
                  pl.BlockSpec((tb, 8, LANES), lambda i: (i, 0, 0)),
                  pl.BlockSpec((tb, SLOTS), lambda i: (i, 0)),
                  full(tab), pl.BlockSpec((8, G_ROWS16), lambda i: (0, 0)),
                  pl.BlockSpec((G_ROWS16, SLOTS), lambda i: (0, 0))],
        out_specs=pl.BlockSpec((tb, SLOTS), lambda i: (i, 0)),
        out_shape=jax.ShapeDtypeStruct((T, SLOTS), f32),
        scratch_shapes=[pltpu.VMEM((TOKENS_PER_STEP_U, G_ROWS, LANES), i32), pltpu.VMEM((tb, G_ROWS16), f32)],
        compiler_params=_params("arbitrary", vmem=TABLE_VMEM_LIMIT),
        name="peer_u",
    )(rows, x1r, gate, tab, jnp.asarray(_chunk_mask()), jnp.asarray(fold, dtype=jnp.bfloat16))


def _peer_v_body(rows_ref, w_ref, x_ref, tab_ref, exp_ref, mask_ref, g_ref, b_ref, o_ref, g_scr, wexp_scr, y_scr):
    tb = x_ref.shape[0]
    wexp_scr[...] = _dot_select(w_ref[...], exp_ref)

    def issue(t, gb):
        return jnp.dot(_split2(mask_ref[...] * wexp_scr[pl.ds(t, 1), :]), gb, preferred_element_type=f32)

    def finish(t, z):
        y_scr[t] = z[0:8] + z[8:16]

    _token_pipeline(tb, rows_ref, tab_ref, g_scr, issue, finish)
    y = ALPHA * x_ref[...] + y_scr[...]
    n_el = float(D_MODEL)
    mu = jnp.sum(y, axis=(1, 2), keepdims=True) / n_el
    yc = y - mu
    var = jnp.sum(yc * yc, axis=(1, 2), keepdims=True) / n_el
    y_scr[...] = yc * lax.rsqrt(var + LN_EPS) * g_ref[...] + b_ref[...]
    for q in range(D_MODEL // LANES):
        o_ref[:, q * LANES:(q + 1) * LANES] = y_scr[:, q, :]


def _peer_v(rows, w, x1r, tab, g, b, tb=512):
    T = rows.shape[0]
    expand = (np.arange(SLOTS)[:, None] == np.arange(G_ROWS16)[None, :] // 8).astype(np.float32)
    full = lambda a: pl.BlockSpec(a.shape, lambda i: (0, 0))
    return pl.pallas_call(
        _peer_v_body,
        grid=(T // tb,),
        in_specs=[pl.BlockSpec((tb, SLOTS), lambda i: (i, 0), memory_space=pltpu.SMEM),
                  pl.BlockSpec((tb, SLOTS), lambda i: (i, 0)),
                  pl.BlockSpec((tb, 8, LANES), lambda i: (i, 0, 0)),
                  full(tab), pl.BlockSpec((SLOTS, G_ROWS16), lambda i: (0, 0)),
                  pl.BlockSpec((8, G_ROWS16), lambda i: (0, 0)),
                  pl.BlockSpec((8, LANES), lambda i: (0, 0)),
                  pl.BlockSpec((8, LANES), lambda i: (0, 0))],
        out_specs=pl.BlockSpec((tb, D_MODEL), lambda i: (i, 0)),
        out_shape=jax.ShapeDtypeStruct((T, D_MODEL), f32),
        scratch_shapes=[pltpu.VMEM((TOKENS_PER_STEP_V, G_ROWS, LANES), i32), pltpu.VMEM((tb, G_ROWS16), f32),
                        pltpu.VMEM((tb, 8, LANES), f32)],
        compiler_params=_params("arbitrary", vmem=TABLE_VMEM_LIMIT),
        name="peer_v",
    )(rows, w, x1r, tab, jnp.asarray(expand, dtype=jnp.bfloat16), jnp.asarray(_chunk_mask()), g, b)


def _regroup_w_in(w):
    offs = np.cumsum(np.array(IN_SPLITS))[:-1].tolist()
    qa, ckv, qi, ki, wi, qb, fb, ib, gb, gate_a, gate_b = jnp.split(w, offs, axis=-1)
    pad = jnp.zeros((w.shape[0], LANES - IDX_DIM - IDX_HEADS), w.dtype)
    out = jnp.concatenate([qa, qi, qb, fb, ib, gb, gate_a, gate_b, ckv, ki, wi, pad], axis=-1)
    assert out.shape[-1] == PROJ_COLS
    return out.astype(bf16)


def kernel(x, w_in, kv_norm_g, w_uk, w_uv, rel_bias, lb_params, b_norm_g, w_br_a, w_br_b, w_o, ln1_g, ln1_b,
           w_pq, sub_keys1, sub_keys2, u_table, v_table, ln2_g, ln2_b):
    B, S, D = x.shape
    T = B * S
    lower_bounds = jnp.cumsum(jax.nn.softmax(lb_params.astype(f32), axis=0), axis=0)
    x2 = x.reshape(T, D)
    for l in range(DEPTH):
        proj = _project(x2, _regroup_w_in(w_in[l]))
        oa = _dsa(proj, kv_norm_g[l][None, :], w_uk[l], w_uv[l], rel_bias, B, S)
        ob = _hgrn(proj, lower_bounds[l][None, :], b_norm_g[l][None, :], B, S)
        x1, x1r = _merge(oa, ob, proj, x2, w_br_a[l].astype(bf16), w_br_b[l].astype(bf16), w_o[l].astype(bf16),
                         ln1_g[l][None, :], ln1_b[l][None, :])
        ids, gate = _route(x1, w_pq[l].astype(bf16), sub_keys1[l].astype(bf16), sub_keys2[l].astype(bf16))
        w = _peer_u(ids, x1r, gate, _pack_table(u_table[l]))
        x2 = _peer_v(ids, w, x1r, _pack_table(v_table[l]), ln2_g[l].reshape(8, LANES),
                     ln2_b[l].reshape(8, LANES)).reshape(T, D)
    return x2.reshape(B, S, D)
```
